```python
import math
import jax, jax.numpy as jnp
from jax import lax
import numpy as np

D_MODEL = 4096
BATCH = 4
SEQ = 4096
DEPTH = 1

EPS = 1e-6
N_META = 16
GRID_W = 64
D_MIX = D_MODEL
HEAD_DIM = 128
D_ATTN = D_MIX // 2
N_Q = D_ATTN // HEAD_DIM
N_KV = 4
Q_PER_KV = N_Q // N_KV
D_KV = N_KV * HEAD_DIM
Q_BLOCK = 128
ATTN_SCALE = 1.0 / math.sqrt(HEAD_DIM)
ROPE_THETA = 10000.0
ROPE_PAIRS = HEAD_DIM // 4
D_RNN = D_MIX - D_ATTN
LRU_BLOCKS = 16
LRU_BW = D_RNN // LRU_BLOCKS
LRU_C = 8.0
CONV_W = 4
CONV_PAD_L = 2
OFF_K = D_ATTN
OFF_V = OFF_K + D_KV
OFF_X = OFF_V + D_KV
OFF_Y = OFF_X + D_RNN
IN_COLS = OFF_Y + D_RNN
PEER_HEADS = 8
N_KEYS = 128
N_EXPERTS = N_KEYS * N_KEYS
PEER_DQ = 256
DQ_HALF = PEER_DQ // 2
PEER_TOPK = 16
PEER_E = PEER_HEADS * PEER_TOPK
PEER_CHUNK = 64

kernel_name = "hymba_attn_rglru_peer_encoder_layer"


def rmsnorm(x, g):
    xf = x.astype(jnp.float32)
    y = xf * lax.rsqrt(jnp.mean(xf * xf, axis=-1, keepdims=True) + EPS)
    return (y * g.astype(jnp.float32)).astype(x.dtype)


def axial_rope_tables(n_tok):
    rows = n_tok // GRID_W
    row = jnp.repeat(jnp.arange(rows, dtype=jnp.float32), GRID_W)
    col = (jnp.arange(rows * GRID_W) % GRID_W).astype(jnp.float32)
    inv = ROPE_THETA ** (-jnp.arange(ROPE_PAIRS, dtype=jnp.float32) / ROPE_PAIRS)
    ar = row[:, None] * inv[None, :]
    ac = col[:, None] * inv[None, :]
    ang = jnp.concatenate([ar, ar, ac, ac], axis=-1)
    ang = jnp.concatenate([jnp.zeros((N_META, HEAD_DIM), jnp.float32), ang], axis=0)
    return jnp.cos(ang), jnp.sin(ang)


def apply_rope(t, cos, sin):
    p = ROPE_PAIRS
    rot = jnp.concatenate([-t[..., p:2 * p], t[..., :p], -t[..., 3 * p:], t[..., 2 * p:3 * p]], axis=-1)
    c = cos[None, :, None, :]
    s = sin[None, :, None, :]
    return (t.astype(jnp.float32) * c + rot.astype(jnp.float32) * s).astype(t.dtype)


def attention_group(q, k, v):
    B, L = q.shape[0], q.shape[1]
    q = q.reshape(B, L, N_KV, Q_PER_KV, HEAD_DIM)

    def attend(qb):
        s = jnp.einsum('bqkgd,bskd->bkgqs', qb, k).astype(jnp.float32) * ATTN_SCALE
        p = jax.nn.softmax(s, axis=-1).astype(v.dtype)
        return jnp.einsum('bkgqs,bskd->bqkgd', p, v)

    o_meta = attend(q[:, :N_META])
    n_real = L - N_META
    n_blk = n_real // Q_BLOCK
    qr = q[:, N_META:].reshape(B, n_blk, Q_BLOCK, N_KV, Q_PER_KV, HEAD_DIM).transpose(1, 0, 2, 3, 4, 5)
    o_real = lax.map(attend, qr).transpose(1, 0, 2, 3, 4, 5).reshape(B, n_real, N_KV, Q_PER_KV, HEAD_DIM)
    o = jnp.concatenate([o_meta, o_real], axis=1)
    return o.reshape(B, L, D_ATTN)


def short_conv(x, w, b):
    y = lax.conv_general_dilated(
        x, w[:, None, :].astype(x.dtype), window_strides=(1,),
        padding=[(CONV_PAD_L, CONV_W - 1 - CONV_PAD_L)],
        dimension_numbers=('NWC', 'WIO', 'NWC'), feature_group_count=x.shape[-1])
    return y + b.astype(x.dtype)


def _lin_combine(left, right):
    a_l, b_l = left
    a_r, b_r = right
    return a_l * a_r, a_r * b_l + b_r


def rg_lru_scan(xc, w_r, b_r, w_i, b_i, lam, reverse):
    B, L = xc.shape[0], xc.shape[1]
    xb = xc.reshape(B, L, LRU_BLOCKS, LRU_BW)
    gate_r = jnp.einsum('blnc,ncd->blnd', xb, w_r).reshape(B, L, D_RNN) + b_r
    gate_i = jnp.einsum('blnc,ncd->blnd', xb, w_i).reshape(B, L, D_RNN) + b_i
    r = jax.nn.sigmoid(gate_r.astype(jnp.float32))
    i = jax.nn.sigmoid(gate_i.astype(jnp.float32))
    log_a = -LRU_C * r * jax.nn.softplus(-lam.astype(jnp.float32))
    a = jnp.exp(log_a)
    b = jnp.sqrt(-jnp.expm1(2.0 * log_a)) * (i * xc.astype(jnp.float32))
    _, h = lax.associative_scan(_lin_combine, (a, b), axis=1, reverse=reverse)
    return h


def peer_ffn(xn, wq, subkeys, u, v):
    B, L, D = xn.shape
    T = B * L
    t = xn.reshape(T, D)
    q = (t @ wq).reshape(T, PEER_HEADS, 2, DQ_HALF)
    s = jnp.einsum('thpc,hpnc->thpn', q, subkeys).astype(jnp.float32)
    s1, i1 = lax.top_k(s[:, :, 0], PEER_TOPK)
    s2, i2 = lax.top_k(s[:, :, 1], PEER_TOPK)
    cand_s = (s1[..., :, None] + s2[..., None, :]).reshape(T, PEER_HEADS, PEER_TOPK * PEER_TOPK)
    cand_i = (i1[..., :, None] * N_KEYS + i2[..., None, :]).reshape(T, PEER_HEADS, PEER_TOPK * PEER_TOPK)
    best_s, pos = lax.top_k(cand_s, PEER_TOPK)
    idx = jnp.take_along_axis(cand_i, pos, axis=-1).reshape(T, PEER_E)
    gate = jax.nn.softmax(best_s, axis=-1).reshape(T, PEER_E)

    pad = (-T) % PEER_CHUNK
    t_p = jnp.pad(t, ((0, pad), (0, 0)))
    idx_p = jnp.pad(idx, ((0, pad), (0, 0)))
    gate_p = jnp.pad(gate, ((0, pad), (0, 0)))
    n_c = (T + pad) // PEER_CHUNK

    def chunk(args):
        tc, ic, gc = args
        uc = u[ic]
        act = jax.nn.gelu(jnp.einsum('ced,cd->ce', uc, tc))
        w = (gc * act.astype(jnp.float32)).astype(tc.dtype)
        vc = v[ic]
        return jnp.einsum('ce,ced->cd', w, vc)

    out = lax.map(chunk, (t_p.reshape(n_c, PEER_CHUNK, D),
                          idx_p.reshape(n_c, PEER_CHUNK, PEER_E),
                          gate_p.reshape(n_c, PEER_CHUNK, PEER_E)))
    return out.reshape(T + pad, D)[:T].reshape(B, L, D)


def setup_inputs(seed: int = 0) -> dict:
    key = jax.random.key(seed)
    ks = jax.random.split(key, 24)
    f32 = jnp.float32

    def nrm(k, shape, scale):
        return jax.random.normal(k, shape, f32) * scale

    def gain(k, shape):
        return 1.0 + 0.05 * jax.random.normal(k, shape, f32)

    u01 = jax.random.uniform(ks[12], (DEPTH, 2, D_RNN), f32, minval=0.9, maxval=0.999)
    sig = u01 ** (1.0 / LRU_C)
    lru_lambda = jnp.log(sig) - jnp.log1p(-sig)

    return {
        "x": nrm(ks[0], (BATCH, SEQ, D_MODEL), 1.0),
        "meta_tokens": nrm(ks[1], (N_META, D_MODEL), 1.0),
        "norm1_g": gain(ks[2], (DEPTH, D_MODEL)),
        "w_in": nrm(ks[3], (DEPTH, D_MODEL, IN_COLS), D_MODEL ** -0.5),
        "q_norm_g": gain(ks[4], (DEPTH, HEAD_DIM)),
        "k_norm_g": gain(ks[5], (DEPTH, HEAD_DIM)),
        "conv_w": nrm(ks[6], (DEPTH, CONV_W, D_RNN), CONV_W ** -0.5),
        "conv_b": nrm(ks[7], (DEPTH, D_RNN), 0.01),
        "w_rg": nrm(ks[8], (DEPTH, 2, LRU_BLOCKS, LRU_BW, LRU_BW), LRU_BW ** -0.5),
        "b_rg": nrm(ks[9], (DEPTH, 2, D_RNN), 0.01),
        "w_ig": nrm(ks[10], (DEPTH, 2, LRU_BLOCKS, LRU_BW, LRU_BW), LRU_BW ** -0.5),
        "b_ig": nrm(ks[11], (DEPTH, 2, D_RNN), 0.01),
        "lru_lambda": lru_lambda,
        "attn_out_g": gain(ks[13], (DEPTH, D_ATTN)),
        "lru_out_g": gain(ks[14], (DEPTH, D_RNN)),
        "w_out": nrm(ks[15], (DEPTH, D_MIX, D_MODEL), D_MIX ** -0.5),
        "norm2_g": gain(ks[16], (DEPTH, D_MODEL)),
        "peer_wq": nrm(ks[17], (DEPTH, D_MODEL, PEER_HEADS * PEER_DQ), D_MODEL ** -0.5),
        "peer_subkeys": nrm(ks[18], (DEPTH, PEER_HEADS, 2, N_KEYS, DQ_HALF), DQ_HALF ** -0.5),
        "peer_u": nrm(ks[19], (DEPTH, N_EXPERTS, D_MODEL), D_MODEL ** -0.5),
        "peer_v": nrm(ks[20], (DEPTH, N_EXPERTS, D_MODEL), PEER_E ** -0.5),
    }


def reference(x, meta_tokens, norm1_g, w_in, q_norm_g, k_norm_g, conv_w, conv_b,
              w_rg, b_rg, w_ig, b_ig, lru_lambda, attn_out_g, lru_out_g, w_out,
              norm2_g, peer_wq, peer_subkeys, peer_u, peer_v):
    B, S, D = x.shape
    meta = jnp.broadcast_to(meta_tokens[None].astype(x.dtype), (B, N_META, D))
    h = jnp.concatenate([meta, x], axis=1)
    L = N_META + S
    cos, sin = axial_rope_tables(S)

    for l in range(DEPTH):
        xn = rmsnorm(h, norm1_g[l])
        z = xn @ w_in[l]
        q = z[..., :OFF_K].reshape(B, L, N_Q, HEAD_DIM)
        k = z[..., OFF_K:OFF_V].reshape(B, L, N_KV, HEAD_DIM)
        v = z[..., OFF_V:OFF_X].reshape(B, L, N_KV, HEAD_DIM)
        xr = z[..., OFF_X:OFF_Y]
        yr = z[..., OFF_Y:]

        q = apply_rope(rmsnorm(q, q_norm_g[l]), cos, sin)
        k = apply_rope(rmsnorm(k, k_norm_g[l]), cos, sin)
        attn = attention_group(q, k, v)

        xc = short_conv(xr, conv_w[l], conv_b[l])
        h_fwd = rg_lru_scan(xc, w_rg[l, 0], b_rg[l, 0], w_ig[l, 0], b_ig[l, 0], lru_lambda[l, 0], False)
        h_bwd = rg_lru_scan(xc, w_rg[l, 1], b_rg[l, 1], w_ig[l, 1], b_ig[l, 1], lru_lambda[l, 1], True)
        lru = ((h_fwd + h_bwd) * jax.nn.gelu(yr.astype(jnp.float32))).astype(h.dtype)

        merged = jnp.concatenate([rmsnorm(attn, attn_out_g[l]), rmsnorm(lru, lru_out_g[l])], axis=-1)
        h = h + merged @ w_out[l]

        h = h + peer_ffn(rmsnorm(h, norm2_g[l]), peer_wq[l], peer_subkeys[l], peer_u[l], peer_v[l])

    return h[:, N_META:]
```

```python
import functools
import math

import jax
import jax.numpy as jnp
from jax import lax
from jax.experimental import pallas as pl
from jax.experimental.pallas import tpu as pltpu

F32 = jnp.float32
BF16 = jnp.bfloat16
I32 = jnp.int32

EPS = 1e-6
N_META = 16
GRID_W = 64
HEAD_DIM = 128
N_KV = 4
Q_PER_KV = 4
ROPE_THETA = 10000.0
ROPE_PAIRS = HEAD_DIM // 4
LRU_BW = 128
LRU_C = 8.0
CONV_W = 4
PEER_HEADS = 8
N_KEYS = 128
PEER_TOPK = 16
ATTN_SCALE = 1.0 / math.sqrt(HEAD_DIM)
NEG_BIG = -1e30

VMEM_LIMIT = 56 * 1024 * 1024


def _cparams(sem):
    return pltpu.CompilerParams(dimension_semantics=sem, vmem_limit_bytes=VMEM_LIMIT)


def _rms(x, g):
    ms = jnp.mean(x * x, axis=-1, keepdims=True)
    return x * lax.rsqrt(ms + EPS) * g


def _gelu(x):
    c = math.sqrt(2.0 / math.pi)
    return 0.5 * x * (1.0 + jnp.tanh(c * (x + 0.044715 * (x * x * x))))


def _sigmoid(x):
    return 1.0 / (1.0 + jnp.exp(-x))


def _dot_nt(a, b):
    return lax.dot_general(a, b, (((1,), (1,)), ((), ())), preferred_element_type=F32)


def _in_proj_kernel(x_ref, g_ref, w_ref, o_ref, xn_ref):
    @pl.when(pl.program_id(1) == 0)
    def _():
        xn_ref[...] = _rms(x_ref[...], g_ref[...]).astype(BF16)

    o_ref[...] = jnp.dot(xn_ref[...], w_ref[...], preferred_element_type=F32)


def _in_proj(x2d, g, w, tm, tn):
    t, d = x2d.shape
    n = w.shape[1]
    return pl.pallas_call(
        _in_proj_kernel,
        out_shape=jax.ShapeDtypeStruct((t, n), F32),
        grid=(t // tm, n // tn),
        in_specs=[pl.BlockSpec((tm, d), lambda i, j: (i, 0)),
                  pl.BlockSpec((1, d), lambda i, j: (0, 0)),
                  pl.BlockSpec((d, tn), lambda i, j: (0, j))],
        out_specs=pl.BlockSpec((tm, tn), lambda i, j: (i, j)),
        scratch_shapes=[pltpu.VMEM((tm, d), BF16)],
        compiler_params=_cparams(("parallel", "arbitrary")),
        name="in_proj",
    )(x2d, g, w)


def _qk_prep_kernel(zqk_ref, zv_ref, cos_ref, sin_ref, qg_ref, kg_ref,
                    q_ref, k_ref, v_ref, *, n_q):
    cos = cos_ref[...]
    sin = sin_ref[...]
    lane = lax.broadcasted_iota(I32, cos.shape, 1)
    lo = (lane % (2 * ROPE_PAIRS)) < ROPE_PAIRS

    def norm_rope(t, g):
        tn = _rms(t, g)
        rot = jnp.where(lo, pltpu.roll(tn, HEAD_DIM - ROPE_PAIRS, 1), pltpu.roll(tn, ROPE_PAIRS, 1))
        return (tn * cos + rot * sin).astype(BF16)

    for h in range(n_q):
        sl = slice(h * HEAD_DIM, (h + 1) * HEAD_DIM)
        q_ref[:, sl] = norm_rope(zqk_ref[:, sl], qg_ref[...])
    for h in range(N_KV):
        src = slice((n_q + h) * HEAD_DIM, (n_q + h + 1) * HEAD_DIM)
        k_ref[:, h * HEAD_DIM:(h + 1) * HEAD_DIM] = norm_rope(zqk_ref[:, src], kg_ref[...])
    v_ref[...] = zv_ref[...].astype(BF16)


def _qk_prep(z, cos, sin_signed, qg, kg, tm, d_attn):
    t = z.shape[0]
    n_q = d_attn // HEAD_DIM
    d_kv = N_KV * HEAD_DIM
    w_qk = d_attn + d_kv
    n_tab = cos.shape[0] // tm
    return pl.pallas_call(
        functools.partial(_qk_prep_kernel, n_q=n_q),
        out_shape=(jax.ShapeDtypeStruct((t, d_attn), BF16),
                   jax.ShapeDtypeStruct((t, d_kv), BF16),
                   jax.ShapeDtypeStruct((t, d_kv), BF16)),
        grid=(t // tm,),
        in_specs=[pl.BlockSpec((tm, w_qk), lambda i: (i, 0)),
                  pl.BlockSpec((tm, d_kv), lambda i: (i, w_qk // d_kv)),
                  pl.BlockSpec((tm, HEAD_DIM), lambda i: (i % n_tab, 0)),
                  pl.BlockSpec((tm, HEAD_DIM), lambda i: (i % n_tab, 0)),
                  pl.BlockSpec((1, HEAD_DIM), lambda i: (0, 0)),
                  pl.BlockSpec((1, HEAD_DIM), lambda i: (0, 0))],
        out_specs=(pl.BlockSpec((tm, d_attn), lambda i: (i, 0)),
                   pl.BlockSpec((tm, d_kv), lambda i: (i, 0)),
                   pl.BlockSpec((tm, d_kv), lambda i: (i, 0))),
        compiler_params=_cparams(("parallel",)),
        name="qk_prep",
    )(z, z, cos, sin_signed, qg, kg)


def _attn_kernel(q_ref, k_ref, v_ref, km_ref, vm_ref, o_ref):
    k = k_ref[...]
    v = v_ref[...]
    km = km_ref[...]
    vm = vm_ref[...]
    col = lax.broadcasted_iota(I32, (1, km.shape[0]), 1)
    meta_bias = jnp.where(col < N_META, 0.0, NEG_BIG).astype(F32)
    for g in range(Q_PER_KV):
        sl = slice(g * HEAD_DIM, (g + 1) * HEAD_DIM)
        q = q_ref[:, sl]
        s = _dot_nt(q, k) * ATTN_SCALE
        sm = _dot_nt(q, km) * ATTN_SCALE + meta_bias
        m = jnp.maximum(jnp.max(s, axis=-1, keepdims=True), jnp.max(sm, axis=-1, keepdims=True))
        p = jnp.exp(s - m)
        pm = jnp.exp(sm - m)
        l = jnp.sum(p, axis=-1, keepdims=True) + jnp.sum(pm, axis=-1, keepdims=True)
        o = (jnp.dot(p.astype(BF16), v, preferred_element_type=F32)
             + jnp.dot(pm.astype(BF16), vm, preferred_element_type=F32))
        o_ref[:, sl] = o / l


def _attention(q, k, v, km, vm, b, s, tq):
    t, d_attn = q.shape
    wq = Q_PER_KV * HEAD_DIM
    nq = s // tq
    mp = km.shape[0]
    return pl.pallas_call(
        _attn_kernel,
        out_shape=jax.ShapeDtypeStruct((t, d_attn), F32),
        grid=(b, N_KV, nq),
        in_specs=[pl.BlockSpec((tq, wq), lambda bi, kh, qi: (bi * nq + qi, kh)),
                  pl.BlockSpec((s, HEAD_DIM), lambda bi, kh, qi: (bi, kh)),
                  pl.BlockSpec((s, HEAD_DIM), lambda bi, kh, qi: (bi, kh)),
                  pl.BlockSpec((mp, HEAD_DIM), lambda bi, kh, qi: (0, kh)),
                  pl.BlockSpec((mp, HEAD_DIM), lambda bi, kh, qi: (0, kh))],
        out_specs=pl.BlockSpec((tq, wq), lambda bi, kh, qi: (bi * nq + qi, kh)),
        compiler_params=_cparams(("parallel", "parallel", "arbitrary")),
        name="attention",
    )(q, k, v, km, vm)


def _lru_kernel(xr_ref, yr_ref, xm_ref, cw_ref, cb_ref, wg_ref, bg_ref, lam_ref, o_ref,
                xpad, af, bf, ab, bb, *, s, rc):
    l = s + N_META
    bw = LRU_BW
    xpad[0:8, :] = jnp.zeros((8, bw), F32)
    xpad[8:8 + N_META, :] = xm_ref[...]
    xpad[8 + N_META:8 + l, :] = xr_ref[...]
    xpad[8 + l:16 + l, :] = jnp.zeros((8, bw), F32)

    lam = lam_ref[...]
    neg = -lam
    sp = jnp.maximum(neg, 0.0) + jnp.log1p(jnp.exp(-jnp.abs(neg)))
    cw = cw_ref[...]
    cb = cb_ref[...]
    wg = wg_ref[...]
    bg = bg_ref[...]

    def gates(t0, n):
        w = xpad[pl.ds(t0, n + 16), :]
        xc = (cw[0:1] * w[6:6 + n] + cw[1:2] * w[7:7 + n]
              + cw[2:3] * w[8:8 + n] + cw[3:4] * w[9:9 + n]) + cb
        gt = jnp.dot(xc.astype(BF16), wg, preferred_element_type=F32) + bg
        for d, (a_ref, b_ref) in enumerate(((af, bf), (ab, bb))):
            r = _sigmoid(gt[:, (2 * d) * bw:(2 * d + 1) * bw])
            i = _sigmoid(gt[:, (2 * d + 1) * bw:(2 * d + 2) * bw])
            log_a = (-LRU_C) * r * sp[d:d + 1]
            a = jnp.exp(log_a)
            bv = jnp.sqrt(1.0 - a * a) * (i * xc)
            a_ref[pl.ds(t0, n), :] = a
            b_ref[pl.ds(t0, n), :] = bv

    gates(0, N_META)

    def gate_body(c, carry):
        gates(pl.multiple_of(N_META + c * rc, 8), rc)
        return carry

    lax.fori_loop(0, s // rc, gate_body, 0)

    def scan_body(t, carry):
        hf, hb = carry
        hf = af[pl.ds(t, 1), :] * hf + bf[pl.ds(t, 1), :]
        bf[pl.ds(t, 1), :] = hf
        tb = l - 1 - t
        hb = ab[pl.ds(tb, 1), :] * hb + bb[pl.ds(tb, 1), :]
        bb[pl.ds(tb, 1), :] = hb
        return hf, hb

    zero = jnp.zeros((1, bw), F32)
    lax.fori_loop(0, l, scan_body, (zero, zero), unroll=8)

    for c in range(s // rc):
        r0 = c * rc
        hsum = bf[N_META + r0:N_META + r0 + rc, :] + bb[N_META + r0:N_META + r0 + rc, :]
        o_ref[r0:r0 + rc, :] = hsum * _gelu(yr_ref[r0:r0 + rc, :])


def _lru(z, zm, cw, cb, wg, bg, lam, b, s, off_x, off_y, d_rnn):
    nblk = d_rnn // LRU_BW
    rc = min(256, s)
    l = s + N_META
    bx = off_x // LRU_BW
    by = off_y // LRU_BW
    return pl.pallas_call(
        functools.partial(_lru_kernel, s=s, rc=rc),
        out_shape=jax.ShapeDtypeStruct((b * s, d_rnn), F32),
        grid=(b, nblk),
        in_specs=[pl.BlockSpec((s, LRU_BW), lambda bi, n: (bi, bx + n)),
                  pl.BlockSpec((s, LRU_BW), lambda bi, n: (bi, by + n)),
                  pl.BlockSpec((N_META, LRU_BW), lambda bi, n: (0, bx + n)),
                  pl.BlockSpec((CONV_W, LRU_BW), lambda bi, n: (0, n)),
                  pl.BlockSpec((1, LRU_BW), lambda bi, n: (0, n)),
                  pl.BlockSpec((None, LRU_BW, 4 * LRU_BW), lambda bi, n: (n, 0, 0)),
                  pl.BlockSpec((None, 1, 4 * LRU_BW), lambda bi, n: (n, 0, 0)),
                  pl.BlockSpec((2, LRU_BW), lambda bi, n: (0, n))],
        out_specs=pl.BlockSpec((s, LRU_BW), lambda bi, n: (bi, n)),
        scratch_shapes=[pltpu.VMEM((l + 16, LRU_BW), F32)] + [pltpu.VMEM((l, LRU_BW), F32)] * 4,
        compiler_params=_cparams(("parallel", "parallel")),
        name="rg_lru",
    )(z, z, zm, cw, cb, wg, bg, lam)


def _out_proj_kernel(a_ref, r_ref, ga_ref, gr_ref, w_ref, x_ref, o_ref, m_ref):
    da = a_ref.shape[1]

    @pl.when(pl.program_id(1) == 0)
    def _():
        m_ref[:, 0:da] = _rms(a_ref[...], ga_ref[...]).astype(BF16)
        m_ref[:, da:] = _rms(r_ref[...], gr_ref[...]).astype(BF16)

    o_ref[...] = x_ref[...] + jnp.dot(m_ref[...], w_ref[...], preferred_element_type=F32)


def _out_proj(attn, lru, ga, gr, w, x2d, tm, tn):
    t, da = attn.shape
    dr = lru.shape[1]
    d = w.shape[1]
    return pl.pallas_call(
        _out_proj_kernel,
        out_shape=jax.ShapeDtypeStruct((t, d), F32),
        grid=(t // tm, d // tn),
        in_specs=[pl.BlockSpec((tm, da), lambda i, j: (i, 0)),
                  pl.BlockSpec((tm, dr), lambda i, j: (i, 0)),
                  pl.BlockSpec((1, da), lambda i, j: (0, 0)),
                  pl.BlockSpec((1, dr), lambda i, j: (0, 0)),
                  pl.BlockSpec((da + dr, tn), lambda i, j: (0, j)),
                  pl.BlockSpec((tm, tn), lambda i, j: (i, j))],
        out_specs=pl.BlockSpec((tm, tn), lambda i, j: (i, j)),
        scratch_shapes=[pltpu.VMEM((tm, da + dr), BF16)],
        compiler_params=_cparams(("parallel", "arbitrary")),
        name="out_proj",
    )(attn, lru, ga, gr, w, x2d)


def _topk_rows(sc, k):
    n, tm = sc.shape
    row = lax.broadcasted_iota(I32, (n, tm), 0)
    cur = sc
    rank = jnp.full((n, tm), k, I32)
    vals = []
    for r in range(k):
        mx = jnp.max(cur, axis=0, keepdims=True)
        idx = jnp.min(jnp.where(cur == mx, row, n), axis=0, keepdims=True)
        sel = row == idx
        rank = jnp.where(sel, r, rank)
        cur = jnp.where(sel, -jnp.inf, cur)
        vals.append(mx)
    return jnp.concatenate(vals, axis=0), rank


def _route_kernel(h_ref, g_ref, wq_ref, sk_ref, xn_ref, e1_ref, m1_ref, e2_ref, b2_ref, q_scr):
    k = PEER_TOPK
    xn = _rms(h_ref[...], g_ref[...]).astype(BF16)
    xn_ref[...] = xn
    q = jnp.dot(xn, wq_ref[...], preferred_element_type=F32).astype(BF16)
    tm = q.shape[0]
    for c in range(2 * PEER_HEADS):
        q_scr[c] = q[:, c * N_KEYS:(c + 1) * N_KEYS]

    hk = k // 2
    pos_col = jnp.concatenate(
        [lax.broadcasted_iota(I32, (k, 1), 0)]
        + [a * k + lax.broadcasted_iota(I32, (hk, 1), 0) for a in range(1, k)], axis=0)
    nc = pos_col.shape[0]
    arow = lax.broadcasted_iota(I32, (k, tm), 0)

    def head_body(h, carry):
        s1 = _dot_nt(sk_ref[2 * h], q_scr[2 * h])
        s2 = _dot_nt(sk_ref[2 * h + 1], q_scr[2 * h + 1])
        v1, r1 = _topk_rows(s1, k)
        v2, r2 = _topk_rows(s2, k)
        cand = jnp.concatenate(
            [v1[0:1] + v2] + [v1[a:a + 1] + v2[0:hk] for a in range(1, k)], axis=0)
        pos = jnp.broadcast_to(pos_col, (nc, tm))
        cur = cand
        mask_a = jnp.zeros((k, tm), I32)
        zsum = jnp.zeros((1, tm), F32)
        top = None
        for j in range(k):
            mx = jnp.max(cur, axis=0, keepdims=True)
            pj = jnp.min(jnp.where(cur == mx, pos, k * k), axis=0, keepdims=True)
            cur = jnp.where(pos == pj, -jnp.inf, cur)
            if j == 0:
                top = mx
            zsum = zsum + jnp.exp(mx - top)
            aj = jnp.right_shift(pj, 4)
            bj = jnp.bitwise_and(pj, k - 1)
            mask_a = jnp.where(arow == aj, jnp.bitwise_or(mask_a, jnp.left_shift(1, bj)), mask_a)
        m1 = jnp.zeros(r1.shape, I32)
        for a in range(k):
            m1 = jnp.where(r1 == a, mask_a[a:a + 1], m1)
        e1_ref[h] = jnp.exp(s1 - v1[0:1] - jnp.log(zsum))
        m1_ref[h] = m1
        e2_ref[h] = jnp.exp(s2 - v2[0:1])
        b2_ref[h] = jnp.left_shift(1, r2)
        return carry

    lax.fori_loop(0, PEER_HEADS, head_body, 0)


def _route(h1, g, wq, sk, tm):
    t, d = h1.shape
    nq = wq.shape[1]
    rt = jax.ShapeDtypeStruct((PEER_HEADS, N_KEYS, t), F32)
    ri = jax.ShapeDtypeStruct((PEER_HEADS, N_KEYS, t), I32)
    rspec = pl.BlockSpec((PEER_HEADS, N_KEYS, tm), lambda i: (0, 0, i))
    return pl.pallas_call(
        _route_kernel,
        out_shape=(jax.ShapeDtypeStruct((t, d), BF16), rt, ri, rt, ri),
        grid=(t // tm,),
        in_specs=[pl.BlockSpec((tm, d), lambda i: (i, 0)),
                  pl.BlockSpec((1, d), lambda i: (0, 0)),
                  pl.BlockSpec((d, nq), lambda i: (0, 0), pipeline_mode=pl.Buffered(1)),
                  pl.BlockSpec((2 * PEER_HEADS, N_KEYS, N_KEYS), lambda i: (0, 0, 0))],
        out_specs=(pl.BlockSpec((tm, d), lambda i: (i, 0)), rspec, rspec, rspec, rspec),
        scratch_shapes=[pltpu.VMEM((2 * PEER_HEADS, tm, N_KEYS), BF16)],
        compiler_params=_cparams(("parallel",)),
        name="peer_route",
    )(h1, g, wq, sk)


def _peer_kernel(xn_ref, h_ref, u_ref, v_ref, e1_ref, m1_ref, e2_ref, b2_ref, o_ref, *, nb):
    e = pl.program_id(1)

    @pl.when(e == 0)
    def _():
        o_ref[...] = h_ref[...]

    act = _dot_nt(u_ref[...], xn_ref[...])
    ga = _gelu(act)
    parts = []
    for kb in range(nb):
        n1 = e * nb + kb
        w = jnp.zeros((N_KEYS, act.shape[1]), F32)
        for h in range(PEER_HEADS):
            m1 = m1_ref[h, pl.ds(n1, 1), :]
            e1 = e1_ref[h, pl.ds(n1, 1), :]
            sel = jnp.bitwise_and(m1, b2_ref[h]) != 0
            w = w + jnp.where(sel, e1 * e2_ref[h], 0.0)
        parts.append((w * ga[kb * N_KEYS:(kb + 1) * N_KEYS, :]).astype(BF16))
    wa = jnp.concatenate(parts, axis=0)
    o_ref[...] += lax.dot_general(wa, v_ref[...], (((0,), (0,)), ((), ())),
                                  preferred_element_type=F32)


def _peer(xn, h1, u, v, e1, m1, e2, b2, tm, ec):
    t, d = xn.shape
    ne = u.shape[0]
    nb = ec // N_KEYS
    one = pl.Buffered(1)
    rspec = pl.BlockSpec((PEER_HEADS, N_KEYS, tm), lambda i, e: (0, 0, i), pipeline_mode=one)
    return pl.pallas_call(
        functools.partial(_peer_kernel, nb=nb),
        out_shape=jax.ShapeDtypeStruct((t, d), F32),
        grid=(t // tm, ne // ec),
        in_specs=[pl.BlockSpec((tm, d), lambda i, e: (i, 0), pipeline_mode=one),
                  pl.BlockSpec((tm, d), lambda i, e: (i, 0), pipeline_mode=one),
                  pl.BlockSpec((ec, d), lambda i, e: (e, 0)),
                  pl.BlockSpec((ec, d), lambda i, e: (e, 0)),
                  rspec, rspec, rspec, rspec],
        out_specs=pl.BlockSpec((tm, d), lambda i, e: (i, 0)),
        compiler_params=_cparams(("parallel", "arbitrary")),
        name="peer_experts",
    )(xn, h1, u, v, e1, m1, e2, b2)


def _rope_tables(s):
    rows = s // GRID_W
    row = jnp.repeat(jnp.arange(rows, dtype=F32), GRID_W)
    col = (jnp.arange(rows * GRID_W) % GRID_W).astype(F32)
    inv = ROPE_THETA ** (-jnp.arange(ROPE_PAIRS, dtype=F32) / ROPE_PAIRS)
    ar = row[:, None] * inv[None, :]
    ac = col[:, None] * inv[None, :]
    ang = jnp.concatenate([ar, ar, ac, ac], axis=-1)
    lane = jnp.arange(HEAD_DIM)
    sign = jnp.where((lane % (2 * ROPE_PAIRS)) < ROPE_PAIRS, -1.0, 1.0).astype(F32)
    return jnp.cos(ang), jnp.sin(ang) * sign[None, :]


def _tile(n, pref):
    t = min(pref, n)
    while n % t:
        t //= 2
    return t


def kernel(x, meta_tokens, norm1_g, w_in, q_norm_g, k_norm_g, conv_w, conv_b, w_rg, b_rg, w_ig,
           b_ig, lru_lambda, attn_out_g, lru_out_g, w_out, norm2_g, peer_wq, peer_subkeys,
           peer_u, peer_v):
    b, s, d = x.shape
    t = b * s
    depth = w_in.shape[0]
    assert depth == 1, "meta rows of the stream are only materialised as attention / recurrence context"
    d_attn = attn_out_g.shape[-1]
    d_rnn = lru_out_g.shape[-1]
    d_kv = N_KV * HEAD_DIM
    off_x = d_attn + 2 * d_kv
    off_y = off_x + d_rnn
    nblk = d_rnn // LRU_BW

    x2d = x.reshape(t, d)
    meta = meta_tokens.astype(x.dtype)
    g1 = norm1_g[0].reshape(1, d)
    w_in_b = w_in[0].astype(BF16)

    tm = _tile(t, 512)
    z = _in_proj(x2d, g1, w_in_b, tm, 1024)
    zm = _in_proj(meta, g1, w_in_b, N_META, 1024)

    cos, sin_s = _rope_tables(s)
    qg = q_norm_g[0].reshape(1, HEAD_DIM)
    kg = k_norm_g[0].reshape(1, HEAD_DIM)
    q, k, v = _qk_prep(z, cos, sin_s, qg, kg, _tile(s, 256), d_attn)
    _, km, vm = _qk_prep(zm, jnp.ones((N_META, HEAD_DIM), F32), jnp.zeros((N_META, HEAD_DIM), F32),
                         qg, kg, N_META, d_attn)
    pad = ((0, N_KEYS - N_META), (0, 0))
    attn = _attention(q, k, v, jnp.pad(km, pad), jnp.pad(vm, pad), b, s, _tile(s, 256))

    wg = jnp.concatenate([w_rg[0, 0], w_ig[0, 0], w_rg[0, 1], w_ig[0, 1]], axis=-1).astype(BF16)
    bg = jnp.stack([b_rg[0, 0], b_ig[0, 0], b_rg[0, 1], b_ig[0, 1]], axis=0)
    bg = bg.reshape(4, nblk, LRU_BW).transpose(1, 0, 2).reshape(nblk, 1, 4 * LRU_BW)
    lru = _lru(z, zm, conv_w[0], conv_b[0].reshape(1, d_rnn), wg, bg, lru_lambda[0],
               b, s, off_x, off_y, d_rnn)

    h1 = _out_proj(attn, lru, attn_out_g[0].reshape(1, d_attn), lru_out_g[0].reshape(1, d_rnn),
                   w_out[0].astype(BF16), x2d, tm, 512)

    sk = peer_subkeys[0].reshape(2 * PEER_HEADS, N_KEYS, -1).astype(BF16)
    xn2, e1, m1, e2, b2 = _route(h1, norm2_g[0].reshape(1, d), peer_wq[0].astype(BF16), sk,
                                 _tile(t, 256))
    out = _peer(xn2, h1, peer_u[0].astype(BF16), peer_v[0].astype(BF16), e1, m1, e2, b2,
                _tile(t, 512), 512)
    return out.reshape(b, s, d)
```

```python
import functools
import math

import jax
import jax.numpy as jnp
from jax import lax
from jax.experimental import pallas as pl
from jax.experimental.pallas import tpu as pltpu

F32 = jnp.float32
BF16 = jnp.bfloat16
I32 = jnp.int32

EPS = 1e-6
N_META = 16
GRID_W = 64
HEAD_DIM = 128
N_KV = 4
Q_PER_KV = 4
ROPE_THETA = 10000.0
ROPE_PAIRS = HEAD_DIM // 4
LRU_BW = 128
LRU_C = 8.0
CONV_W = 4
PEER_HEADS = 8
N_KEYS = 128
PEER_TOPK = 16
ATTN_SCALE = 1.0 / math.sqrt(HEAD_DIM)
NEG_BIG = -1e30

VMEM_LIMIT = 60 * 1024 * 1024


def _cparams(sem, flags=None):
    return pltpu.CompilerParams(dimension_semantics=sem, vmem_limit_bytes=VMEM_LIMIT, flags=flags)


def _rms(x, g):
    ms = jnp.mean(x * x, axis=-1, keepdims=True)
    return x * lax.rsqrt(ms + EPS) * g


def _gelu(x):
    c = math.sqrt(2.0 / math.pi)
    return 0.5 * x * (1.0 + jnp.tanh(c * (x + 0.044715 * (x * x * x))))


def _sigmoid(x):
    return 0.5 * (1.0 + jnp.tanh(0.5 * x))


def _dot_nt(a, b):
    return lax.dot_general(a, b, (((1,), (1,)), ((), ())), preferred_element_type=F32)


def _in_proj_kernel(x_ref, g_ref, w_ref, o_ref, xn_ref):
    @pl.when(pl.program_id(1) == 0)
    def _():
        xn_ref[...] = _rms(x_ref[...], g_ref[...]).astype(BF16)

    o_ref[...] = jnp.dot(xn_ref[...], w_ref[...], preferred_element_type=F32)


def _in_proj(x2d, g, w, tm, tn):
    t, d = x2d.shape
    n = w.shape[1]
    return pl.pallas_call(
        _in_proj_kernel,
        out_shape=jax.ShapeDtypeStruct((t, n), F32),
        grid=(t // tm, n // tn),
        in_specs=[pl.BlockSpec((tm, d), lambda i, j: (i, 0)),
                  pl.BlockSpec((1, d), lambda i, j: (0, 0)),
                  pl.BlockSpec((d, tn), lambda i, j: (0, j))],
        out_specs=pl.BlockSpec((tm, tn), lambda i, j: (i, j)),
        scratch_shapes=[pltpu.VMEM((tm, d), BF16)],
        compiler_params=_cparams(("parallel", "arbitrary")),
        name="in_proj",
    )(x2d, g, w)


def _qk_prep_kernel(zqk_ref, zv_ref, cos_ref, sin_ref, qg_ref, kg_ref,
                    q_ref, k_ref, v_ref, *, n_q):
    cos = cos_ref[...]
    sin = sin_ref[...]
    lane = lax.broadcasted_iota(I32, cos.shape, 1)
    lo = (lane % (2 * ROPE_PAIRS)) < ROPE_PAIRS

    def norm_rope(t, g):
        tn = _rms(t, g)
        rot = jnp.where(lo, pltpu.roll(tn, HEAD_DIM - ROPE_PAIRS, 1), pltpu.roll(tn, ROPE_PAIRS, 1))
        return (tn * cos + rot * sin).astype(BF16)

    for h in range(n_q):
        sl = slice(h * HEAD_DIM, (h + 1) * HEAD_DIM)
        q_ref[:, sl] = norm_rope(zqk_ref[:, sl], qg_ref[...])
    for h in range(N_KV):
        src = slice((n_q + h) * HEAD_DIM, (n_q + h + 1) * HEAD_DIM)
        k_ref[:, h * HEAD_DIM:(h + 1) * HEAD_DIM] = norm_rope(zqk_ref[:, src], kg_ref[...])
    v_ref[...] = zv_ref[...].astype(BF16)


def _qk_prep(z, cos, sin_signed, qg, kg, tm, d_attn):
    t = z.shape[0]
    n_q = d_attn // HEAD_DIM
    d_kv = N_KV * HEAD_DIM
    w_qk = d_attn + d_kv
    n_tab = cos.shape[0] // tm
    return pl.pallas_call(
        functools.partial(_qk_prep_kernel, n_q=n_q),
        out_shape=(jax.ShapeDtypeStruct((t, d_attn), BF16),
                   jax.ShapeDtypeStruct((t, d_kv), BF16),
                   jax.ShapeDtypeStruct((t, d_kv), BF16)),
        grid=(t // tm,),
        in_specs=[pl.BlockSpec((tm, w_qk), lambda i: (i, 0)),
                  pl.BlockSpec((tm, d_kv), lambda i: (i, w_qk // d_kv)),
                  pl.BlockSpec((tm, HEAD_DIM), lambda i: (i % n_tab, 0)),
                  pl.BlockSpec((tm, HEAD_DIM), lambda i: (i % n_tab, 0)),
                  pl.BlockSpec((1, HEAD_DIM), lambda i: (0, 0)),
                  pl.BlockSpec((1, HEAD_DIM), lambda i: (0, 0))],
        out_specs=(pl.BlockSpec((tm, d_attn), lambda i: (i, 0)),
                   pl.BlockSpec((tm, d_kv), lambda i: (i, 0)),
                   pl.BlockSpec((tm, d_kv), lambda i: (i, 0))),
        compiler_params=_cparams(("parallel",)),
        name="qk_prep",
    )(z, z, cos, sin_signed, qg, kg)


def _attn_kernel(q_ref, k_ref, v_ref, km_ref, vm_ref, o_ref):
    k = k_ref[...]
    v = v_ref[...]
    km = km_ref[...]
    vm = vm_ref[...]
    col = lax.broadcasted_iota(I32, (1, km.shape[0]), 1)
    meta_bias = jnp.where(col < N_META, 0.0, NEG_BIG).astype(F32)
    ns = k.shape[0]
    kc = min(ns, 512)
    lw = km.shape[0]
    c = ATTN_SCALE * math.log2(math.e)
    for g in range(Q_PER_KV):
        sl = slice(g * HEAD_DIM, (g + 1) * HEAD_DIM)
        q = q_ref[:, sl]
        s = _dot_nt(q, k)
        sm = _dot_nt(q, km) + meta_bias
        mx = sm
        for j in range(ns // lw):
            mx = jnp.maximum(mx, s[:, j * lw:(j + 1) * lw])
        mc = jnp.max(mx, axis=-1, keepdims=True) * c
        pm = jnp.exp2(sm * c - mc)
        lsum = pm
        o = jnp.dot(pm.astype(BF16), vm, preferred_element_type=F32)
        for j in range(ns // kc):
            parts = []
            for i in range(kc // lw):
                lo = j * kc + i * lw
                p = jnp.exp2(s[:, lo:lo + lw] * c - mc)
                lsum = lsum + p
                parts.append(p.astype(BF16))
            o = o + jnp.dot(jnp.concatenate(parts, axis=1), v[j * kc:(j + 1) * kc, :],
                            preferred_element_type=F32)
        o_ref[:, sl] = o / jnp.sum(lsum, axis=-1, keepdims=True)


def _attention(q, k, v, km, vm, b, s, tq):
    t, d_attn = q.shape
    wq = Q_PER_KV * HEAD_DIM
    nq = s // tq
    mp = km.shape[0]
    return pl.pallas_call(
        _attn_kernel,
        out_shape=jax.ShapeDtypeStruct((t, d_attn), F32),
        grid=(b, N_KV, nq),
        in_specs=[pl.BlockSpec((tq, wq), lambda bi, kh, qi: (bi * nq + qi, kh)),
                  pl.BlockSpec((s, HEAD_DIM), lambda bi, kh, qi: (bi, kh)),
                  pl.BlockSpec((s, HEAD_DIM), lambda bi, kh, qi: (bi, kh)),
                  pl.BlockSpec((mp, HEAD_DIM), lambda bi, kh, qi: (0, kh)),
                  pl.BlockSpec((mp, HEAD_DIM), lambda bi, kh, qi: (0, kh))],
        out_specs=pl.BlockSpec((tq, wq), lambda bi, kh, qi: (bi * nq + qi, kh)),
        compiler_params=_cparams(("parallel", "parallel", "arbitrary")),
        name="attention",
    )(q, k, v, km, vm)


def _lru_kernel(xr_ref, yr_ref, xm_ref, cw_ref, cb_ref, wg_ref, bg_ref, lam_ref, o_ref,
                xpad, af, bf, ab, bb, *, s, rc):
    l = s + N_META
    bw = LRU_BW
    xpad[0:8, :] = jnp.zeros((8, bw), F32)
    xpad[8:8 + N_META, :] = xm_ref[...]
    xpad[8 + N_META:8 + l, :] = xr_ref[...]
    xpad[8 + l:16 + l, :] = jnp.zeros((8, bw), F32)

    lam = lam_ref[...]
    neg = -lam
    sp = jnp.maximum(neg, 0.0) + jnp.log1p(jnp.exp(-jnp.abs(neg)))
    cw = cw_ref[...]
    cb = cb_ref[...]
    wg = wg_ref[...]
    bg = bg_ref[...]

    def gates(t0, n):
        w = xpad[pl.ds(t0, n + 16), :]
        xc = (cw[0:1] * w[6:6 + n] + cw[1:2] * w[7:7 + n]
              + cw[2:3] * w[8:8 + n] + cw[3:4] * w[9:9 + n]) + cb
        gt = jnp.dot(xc.astype(BF16), wg, preferred_element_type=F32) + bg
        for d, (a_ref, b_ref) in enumerate(((af, bf), (ab, bb))):
            r = _sigmoid(gt[:, (2 * d) * bw:(2 * d + 1) * bw])
            i = _sigmoid(gt[:, (2 * d + 1) * bw:(2 * d + 2) * bw])
            log_a = (-LRU_C) * r * sp[d:d + 1]
            a = jnp.exp(log_a)
            bv = jnp.sqrt(1.0 - a * a) * (i * xc)
            a_ref[pl.ds(t0, n), :] = a
            b_ref[pl.ds(t0, n), :] = bv

    gates(0, N_META)

    def gate_body(c, carry):
        gates(pl.multiple_of(N_META + c * rc, 8), rc)
        return carry

    lax.fori_loop(0, s // rc, gate_body, 0)

    nch = 8
    cl = l // nch

    def rows(ref, i):
        return ref.at[pl.ds(i, nch, stride=cl), :]

    def pass1(i, carry):
        hf, pf, hb, pb = carry
        a = rows(af, i)[...]
        hf = a * hf + rows(bf, i)[...]
        pf = a * pf
        rows(bf, i)[...] = hf
        rows(af, i)[...] = pf
        j = cl - 1 - i
        a = rows(ab, j)[...]
        hb = a * hb + rows(bb, j)[...]
        pb = a * pb
        rows(bb, j)[...] = hb
        rows(ab, j)[...] = pb
        return hf, pf, hb, pb

    zero = jnp.zeros((nch, bw), F32)
    one = jnp.ones((nch, bw), F32)
    hf, pf, hb, pb = lax.fori_loop(0, cl, pass1, (zero, one, zero, one), unroll=2)

    cf = [jnp.zeros((1, bw), F32)]
    for c in range(1, nch):
        cf.append(hf[c - 1:c] + pf[c - 1:c] * cf[-1])
    cin_f = jnp.concatenate(cf, axis=0)
    cbk = [jnp.zeros((1, bw), F32)]
    for c in range(nch - 2, -1, -1):
        cbk.append(hb[c + 1:c + 2] + pb[c + 1:c + 2] * cbk[-1])
    cin_b = jnp.concatenate(cbk[::-1], axis=0)

    def pass2(i, carry):
        rows(bf, i)[...] = rows(bf, i)[...] + rows(af, i)[...] * cin_f
        rows(bb, i)[...] = rows(bb, i)[...] + rows(ab, i)[...] * cin_b
        return carry

    lax.fori_loop(0, cl, pass2, 0, unroll=2)

    for c in range(s // rc):
        r0 = c * rc
        hsum = bf[N_META + r0:N_META + r0 + rc, :] + bb[N_META + r0:N_META + r0 + rc, :]
        o_ref[r0:r0 + rc, :] = hsum * _gelu(yr_ref[r0:r0 + rc, :])


def _lru(z, zm, cw, cb, wg, bg, lam, b, s, off_x, off_y, d_rnn):
    nblk = d_rnn // LRU_BW
    rc = min(256, s)
    l = s + N_META
    bx = off_x // LRU_BW
    by = off_y // LRU_BW
    return pl.pallas_call(
        functools.partial(_lru_kernel, s=s, rc=rc),
        out_shape=jax.ShapeDtypeStruct((b * s, d_rnn), F32),
        grid=(b, nblk),
        in_specs=[pl.BlockSpec((s, LRU_BW), lambda bi, n: (bi, bx + n)),
                  pl.BlockSpec((s, LRU_BW), lambda bi, n: (bi, by + n)),
                  pl.BlockSpec((N_META, LRU_BW), lambda bi, n: (0, bx + n)),
                  pl.BlockSpec((CONV_W, LRU_BW), lambda bi, n: (0, n)),
                  pl.BlockSpec((1, LRU_BW), lambda bi, n: (0, n)),
                  pl.BlockSpec((None, LRU_BW, 4 * LRU_BW), lambda bi, n: (n, 0, 0)),
                  pl.BlockSpec((None, 1, 4 * LRU_BW), lambda bi, n: (n, 0, 0)),
                  pl.BlockSpec((2, LRU_BW), lambda bi, n: (0, n))],
        out_specs=pl.BlockSpec((s, LRU_BW), lambda bi, n: (bi, n)),
        scratch_shapes=[pltpu.VMEM((l + 16, LRU_BW), F32)] + [pltpu.VMEM((l, LRU_BW), F32)] * 4,
        compiler_params=_cparams(("parallel", "parallel")),
        name="rg_lru",
    )(z, z, zm, cw, cb, wg, bg, lam)


def _out_proj_kernel(a_ref, r_ref, ga_ref, gr_ref, w_ref, x_ref, o_ref, m_ref):
    da = a_ref.shape[1]

    @pl.when(pl.program_id(1) == 0)
    def _():
        m_ref[:, 0:da] = _rms(a_ref[...], ga_ref[...]).astype(BF16)
        m_ref[:, da:] = _rms(r_ref[...], gr_ref[...]).astype(BF16)

    o_ref[...] = x_ref[...] + jnp.dot(m_ref[...], w_ref[...], preferred_element_type=F32)


def _out_proj(attn, lru, ga, gr, w, x2d, tm, tn):
    t, da = attn.shape
    dr = lru.shape[1]
    d = w.shape[1]
    return pl.pallas_call(
        _out_proj_kernel,
        out_shape=jax.ShapeDtypeStruct((t, d), F32),
        grid=(t // tm, d // tn),
        in_specs=[pl.BlockSpec((tm, da), lambda i, j: (i, 0)),
                  pl.BlockSpec((tm, dr), lambda i, j: (i, 0)),
                  pl.BlockSpec((1, da), lambda i, j: (0, 0)),
                  pl.BlockSpec((1, dr), lambda i, j: (0, 0)),
                  pl.BlockSpec((da + dr, tn), lambda i, j: (0, j)),
                  pl.BlockSpec((tm, tn), lambda i, j: (i, j))],
        out_specs=pl.BlockSpec((tm, tn), lambda i, j: (i, j)),
        scratch_shapes=[pltpu.VMEM((tm, da + dr), BF16)],
        compiler_params=_cparams(("parallel", "arbitrary")),
        name="out_proj",
    )(attn, lru, ga, gr, w, x2d)


def _topk_rows(sc, k):
    n, tm = sc.shape
    row = lax.broadcasted_iota(I32, (n, tm), 0)
    cur = sc
    rank = jnp.full((n, tm), k, I32)
    vals = []
    for r in range(k):
        mx = jnp.max(cur, axis=0, keepdims=True)
        idx = jnp.min(jnp.where(cur == mx, row, n), axis=0, keepdims=True)
        sel = row == idx
        rank = jnp.where(sel, r, rank)
        cur = jnp.where(sel, -jnp.inf, cur)
        vals.append(mx)
    return jnp.concatenate(vals, axis=0), rank


def _route_kernel(h_ref, g_ref, wq_ref, sk_ref, xn_ref, e1_ref, m1_ref, e2_ref, b2_ref, q_scr):
    k = PEER_TOPK
    xn = _rms(h_ref[...], g_ref[...]).astype(BF16)
    xn_ref[...] = xn
    q = jnp.dot(xn, wq_ref[...], preferred_element_type=F32).astype(BF16)
    tm = q.shape[0]
    for c in range(2 * PEER_HEADS):
        q_scr[c] = q[:, c * N_KEYS:(c + 1) * N_KEYS]

    hk = k // 2
    pos_col = jnp.concatenate(
        [lax.broadcasted_iota(I32, (k, 1), 0)]
        + [a * k + lax.broadcasted_iota(I32, (hk, 1), 0) for a in range(1, k)], axis=0)
    nc = pos_col.shape[0]
    arow = lax.broadcasted_iota(I32, (k, tm), 0)

    def head_body(h, carry):
        s1 = _dot_nt(sk_ref[2 * h], q_scr[2 * h])
        s2 = _dot_nt(sk_ref[2 * h + 1], q_scr[2 * h + 1])
        v1, r1 = _topk_rows(s1, k)
        v2, r2 = _topk_rows(s2, k)
        cand = jnp.concatenate(
            [v1[0:1] + v2] + [v1[a:a + 1] + v2[0:hk] for a in range(1, k)], axis=0)
        pos = jnp.broadcast_to(pos_col, (nc, tm))
        cur = cand
        mask_a = jnp.zeros((k, tm), I32)
        zsum = jnp.zeros((1, tm), F32)
        top = None
        for j in range(k):
            mx = jnp.max(cur, axis=0, keepdims=True)
            pj = jnp.min(jnp.where(cur == mx, pos, k * k), axis=0, keepdims=True)
            cur = jnp.where(pos == pj, -jnp.inf, cur)
            if j == 0:
                top = mx
            zsum = zsum + jnp.exp(mx - top)
            aj = jnp.right_shift(pj, 4)
            bj = jnp.bitwise_and(pj, k - 1)
            mask_a = jnp.where(arow == aj, jnp.bitwise_or(mask_a, jnp.left_shift(1, bj)), mask_a)
        m1 = jnp.zeros(r1.shape, I32)
        for a in range(k):
            m1 = jnp.where(r1 == a, mask_a[a:a + 1], m1)
        e1_ref[h] = jnp.exp(s1 - v1[0:1] - jnp.log(zsum))
        m1_ref[h] = m1
        e2_ref[h] = jnp.exp(s2 - v2[0:1])
        b2_ref[h] = jnp.left_shift(1, r2)
        return carry

    lax.fori_loop(0, PEER_HEADS, head_body, 0)


def _route(h1, g, wq, sk, tm):
    t, d = h1.shape
    nq = wq.shape[1]
    rt = jax.ShapeDtypeStruct((PEER_HEADS, N_KEYS, t), F32)
    ri = jax.ShapeDtypeStruct((PEER_HEADS, N_KEYS, t), I32)
    rspec = pl.BlockSpec((PEER_HEADS, N_KEYS, tm), lambda i: (0, 0, i))
    return pl.pallas_call(
        _route_kernel,
        out_shape=(jax.ShapeDtypeStruct((t, d), BF16), rt, ri, rt, ri),
        grid=(t // tm,),
        in_specs=[pl.BlockSpec((tm, d), lambda i: (i, 0)),
                  pl.BlockSpec((1, d), lambda i: (0, 0)),
                  pl.BlockSpec((d, nq), lambda i: (0, 0), pipeline_mode=pl.Buffered(1)),
                  pl.BlockSpec((2 * PEER_HEADS, N_KEYS, N_KEYS), lambda i: (0, 0, 0))],
        out_specs=(pl.BlockSpec((tm, d), lambda i: (i, 0)), rspec, rspec, rspec, rspec),
        scratch_shapes=[pltpu.VMEM((2 * PEER_HEADS, tm, N_KEYS), BF16)],
        compiler_params=_cparams(("parallel",)),
        name="peer_route",
    )(h1, g, wq, sk)


def _peer_kernel(xn_ref, h_ref, u_ref, v_ref, e1_ref, m1_ref, e2_ref, b2_ref, o_ref,
                 *, nb):
    e = pl.program_id(1)

    @pl.when(e == 0)
    def _():
        o_ref[...] = h_ref[...]

    act = jnp.dot(u_ref[...], xn_ref[...], preferred_element_type=F32)
    ga = _gelu(act)
    tm = ga.shape[1]
    parts = []
    for kb in range(nb):
        n1 = e * nb + kb
        w = jnp.zeros((N_KEYS, tm), F32)
        for h in range(PEER_HEADS):
            m1 = m1_ref[h, pl.ds(n1, 1), :]
            e1 = e1_ref[h, pl.ds(n1, 1), :]
            sel = jnp.bitwise_and(m1, b2_ref[h]) != 0
            w = w + jnp.where(sel, e1 * e2_ref[h], 0.0)
        parts.append((w * ga[kb * N_KEYS:(kb + 1) * N_KEYS, :]).T.astype(BF16))
    wa = jnp.concatenate(parts, axis=1)
    o_ref[...] += jnp.dot(wa, v_ref[...], preferred_element_type=F32)


def _peer(xnt, h1, u, v, e1, m1, e2, b2, tm, ec):
    d, t = xnt.shape
    nchunk = u.shape[0] // ec
    nb = ec // N_KEYS
    one = pl.Buffered(1)
    rspec = pl.BlockSpec((PEER_HEADS, N_KEYS, tm), lambda i, e: (0, 0, i), pipeline_mode=one)
    return pl.pallas_call(
        functools.partial(_peer_kernel, nb=nb),
        out_shape=jax.ShapeDtypeStruct((t, d), F32),
        grid=(t // tm, nchunk),
        in_specs=[pl.BlockSpec((d, tm), lambda i, e: (0, i), pipeline_mode=one),
                  pl.BlockSpec((tm, d), lambda i, e: (i, 0), pipeline_mode=one),
                  pl.BlockSpec((ec, d), lambda i, e: (e, 0)),
                  pl.BlockSpec((ec, d), lambda i, e: (e, 0)),
                  rspec, rspec, rspec, rspec],
        out_specs=pl.BlockSpec((tm, d), lambda i, e: (i, 0)),
        compiler_params=_cparams(("parallel", "arbitrary")),
        name="peer_experts",
    )(xnt, h1, u, v, e1, m1, e2, b2)


def _rope_tables(s):
    rows = s // GRID_W
    row = jnp.repeat(jnp.arange(rows, dtype=F32), GRID_W)
    col = (jnp.arange(rows * GRID_W) % GRID_W).astype(F32)
    inv = ROPE_THETA ** (-jnp.arange(ROPE_PAIRS, dtype=F32) / ROPE_PAIRS)
    ar = row[:, None] * inv[None, :]
    ac = col[:, None] * inv[None, :]
    ang = jnp.concatenate([ar, ar, ac, ac], axis=-1)
    lane = jnp.arange(HEAD_DIM)
    sign = jnp.where((lane % (2 * ROPE_PAIRS)) < ROPE_PAIRS, -1.0, 1.0).astype(F32)
    return jnp.cos(ang), jnp.sin(ang) * sign[None, :]


def _tile(n, pref):
    t = min(pref, n)
    while n % t:
        t //= 2
    return t


def kernel(x, meta_tokens, norm1_g, w_in, q_norm_g, k_norm_g, conv_w, conv_b, w_rg, b_rg, w_ig,
           b_ig, lru_lambda, attn_out_g, lru_out_g, w_out, norm2_g, peer_wq, peer_subkeys,
           peer_u, peer_v):
    b, s, d = x.shape
    t = b * s
    depth = w_in.shape[0]
    assert depth == 1, "meta rows of the stream are only materialised as attention / recurrence context"
    d_attn = attn_out_g.shape[-1]
    d_rnn = lru_out_g.shape[-1]
    d_kv = N_KV * HEAD_DIM
    off_x = d_attn + 2 * d_kv
    off_y = off_x + d_rnn
    nblk = d_rnn // LRU_BW

    x2d = x.reshape(t, d)
    meta = meta_tokens.astype(x.dtype)
    g1 = norm1_g[0].reshape(1, d)
    w_in_b = w_in[0].astype(BF16)

    tm = _tile(t, 512)
    z = _in_proj(x2d, g1, w_in_b, tm, 1024)
    zm = _in_proj(meta, g1, w_in_b, N_META, 1024)

    cos, sin_s = _rope_tables(s)
    qg = q_norm_g[0].reshape(1, HEAD_DIM)
    kg = k_norm_g[0].reshape(1, HEAD_DIM)
    q, k, v = _qk_prep(z, cos, sin_s, qg, kg, _tile(s, 256), d_attn)
    _, km, vm = _qk_prep(zm, jnp.ones((N_META, HEAD_DIM), F32), jnp.zeros((N_META, HEAD_DIM), F32),
                         qg, kg, N_META, d_attn)
    pad = ((0, N_KEYS - N_META), (0, 0))
    attn = _attention(q, k, v, jnp.pad(km, pad), jnp.pad(vm, pad), b, s, _tile(s, 256))

    wg = jnp.concatenate([w_rg[0, 0], w_ig[0, 0], w_rg[0, 1], w_ig[0, 1]], axis=-1).astype(BF16)
    bg = jnp.stack([b_rg[0, 0], b_ig[0, 0], b_rg[0, 1], b_ig[0, 1]], axis=0)
    bg = bg.reshape(4, nblk, LRU_BW).transpose(1, 0, 2).reshape(nblk, 1, 4 * LRU_BW)
    lru = _lru(z, zm, conv_w[0], conv_b[0].reshape(1, d_rnn), wg, bg, lru_lambda[0],
               b, s, off_x, off_y, d_rnn)

    h1 = _out_proj(attn, lru, attn_out_g[0].reshape(1, d_attn), lru_out_g[0].reshape(1, d_rnn),
                   w_out[0].astype(BF16), x2d, tm, 512)

    sk = peer_subkeys[0].reshape(2 * PEER_HEADS, N_KEYS, -1).astype(BF16)
    xn2, e1, m1, e2, b2 = _route(h1, norm2_g[0].reshape(1, d), peer_wq[0].astype(BF16), sk,
                                 _tile(t, 256))
    out = _peer(xn2.T, h1, peer_u[0].astype(BF16), peer_v[0].astype(BF16), e1, m1, e2, b2,
                _tile(t, 512), 512)
    return out.reshape(b, s, d)
```

```python
import functools
import math

import jax
import jax.numpy as jnp
from jax import lax
from jax.experimental import pallas as pl
from jax.experimental.pallas import tpu as pltpu

F32 = jnp.float32
BF16 = jnp.bfloat16
I32 = jnp.int32

EPS = 1e-6
N_META = 16
GRID_W = 64
HEAD_DIM = 128
N_KV = 4
Q_PER_KV = 4
ROPE_THETA = 10000.0
ROPE_PAIRS = HEAD_DIM // 4
LRU_BW = 128
LRU_C = 8.0
CONV_W = 4
PEER_HEADS = 8
N_KEYS = 128
PEER_TOPK = 16
ATTN_SCALE = 1.0 / math.sqrt(HEAD_DIM)
NEG_BIG = -1e30

VMEM_LIMIT = 60 * 1024 * 1024


def _cparams(sem, flags=None):
    return pltpu.CompilerParams(dimension_semantics=sem, vmem_limit_bytes=VMEM_LIMIT, flags=flags)


def _rms(x, g):
    ms = jnp.mean(x * x, axis=-1, keepdims=True)
    return x * lax.rsqrt(ms + EPS) * g


def _gelu(x):
    c = math.sqrt(2.0 / math.pi)
    return 0.5 * x * (1.0 + jnp.tanh(c * (x + 0.044715 * (x * x * x))))


def _dot_nt(a, b):
    return lax.dot_general(a, b, (((1,), (1,)), ((), ())), preferred_element_type=F32)


def _in_proj_kernel(x_ref, g_ref, w_ref, o_ref, xn_ref):
    @pl.when(pl.program_id(1) == 0)
    def _():
        xn_ref[...] = _rms(x_ref[...], g_ref[...]).astype(BF16)

    o_ref[...] = jnp.dot(xn_ref[...], w_ref[...], preferred_element_type=F32)


def _in_proj(x2d, g, w, tm, tn):
    t, d = x2d.shape
    n = w.shape[1]
    return pl.pallas_call(
        _in_proj_kernel,
        out_shape=jax.ShapeDtypeStruct((t, n), F32),
        grid=(t // tm, n // tn),
        in_specs=[pl.BlockSpec((tm, d), lambda i, j: (i, 0)),
                  pl.BlockSpec((1, d), lambda i, j: (0, 0)),
                  pl.BlockSpec((d, tn), lambda i, j: (0, j))],
        out_specs=pl.BlockSpec((tm, tn), lambda i, j: (i, j)),
        scratch_shapes=[pltpu.VMEM((tm, d), BF16)],
        compiler_params=_cparams(("parallel", "arbitrary")),
        name="in_proj",
    )(x2d, g, w)


def _qk_prep_kernel(zqk_ref, zv_ref, cos_ref, sin_ref, qg_ref, kg_ref,
                    q_ref, k_ref, v_ref, *, n_q):
    cos = cos_ref[...]
    sin = sin_ref[...]
    lane = lax.broadcasted_iota(I32, cos.shape, 1)
    lo = (lane % (2 * ROPE_PAIRS)) < ROPE_PAIRS

    def norm_rope(t, g):
        tn = _rms(t, g)
        rot = jnp.where(lo, pltpu.roll(tn, HEAD_DIM - ROPE_PAIRS, 1), pltpu.roll(tn, ROPE_PAIRS, 1))
        return (tn * cos + rot * sin).astype(BF16)

    for h in range(n_q):
        sl = slice(h * HEAD_DIM, (h + 1) * HEAD_DIM)
        q_ref[:, sl] = norm_rope(zqk_ref[:, sl], qg_ref[...])
    for h in range(N_KV):
        src = slice((n_q + h) * HEAD_DIM, (n_q + h + 1) * HEAD_DIM)
        k_ref[:, h * HEAD_DIM:(h + 1) * HEAD_DIM] = norm_rope(zqk_ref[:, src], kg_ref[...])
    v_ref[...] = zv_ref[...].astype(BF16)


def _qk_prep(z, cos, sin_signed, qg, kg, tm, d_attn):
    t = z.shape[0]
    n_q = d_attn // HEAD_DIM
    d_kv = N_KV * HEAD_DIM
    w_qk = d_attn + d_kv
    n_tab = cos.shape[0] // tm
    return pl.pallas_call(
        functools.partial(_qk_prep_kernel, n_q=n_q),
        out_shape=(jax.ShapeDtypeStruct((t, d_attn), BF16),
                   jax.ShapeDtypeStruct((t, d_kv), BF16),
                   jax.ShapeDtypeStruct((t, d_kv), BF16)),
        grid=(t // tm,),
        in_specs=[pl.BlockSpec((tm, w_qk), lambda i: (i, 0)),
                  pl.BlockSpec((tm, d_kv), lambda i: (i, w_qk // d_kv)),
                  pl.BlockSpec((tm, HEAD_DIM), lambda i: (i % n_tab, 0)),
                  pl.BlockSpec((tm, HEAD_DIM), lambda i: (i % n_tab, 0)),
                  pl.BlockSpec((1, HEAD_DIM), lambda i: (0, 0)),
                  pl.BlockSpec((1, HEAD_DIM), lambda i: (0, 0))],
        out_specs=(pl.BlockSpec((tm, d_attn), lambda i: (i, 0)),
                   pl.BlockSpec((tm, d_kv), lambda i: (i, 0)),
                   pl.BlockSpec((tm, d_kv), lambda i: (i, 0))),
        compiler_params=_cparams(("parallel",)),
        name="qk_prep",
    )(z, z, cos, sin_signed, qg, kg)


def _attn_kernel(q_ref, k_ref, v_ref, km_ref, vm_ref, o_ref):
    k = k_ref[...]
    v = v_ref[...]
    km = km_ref[...]
    vm = vm_ref[...]
    col = lax.broadcasted_iota(I32, (1, km.shape[0]), 1)
    meta_bias = jnp.where(col < N_META, 0.0, NEG_BIG).astype(F32)
    ns = k.shape[0]
    kc = min(ns, 512)
    lw = km.shape[0]
    c = ATTN_SCALE * math.log2(math.e)
    for g in range(Q_PER_KV):
        sl = slice(g * HEAD_DIM, (g + 1) * HEAD_DIM)
        q = q_ref[:, sl]
        s = _dot_nt(q, k)
        sm = _dot_nt(q, km) + meta_bias
        mx = sm
        for j in range(ns // lw):
            mx = jnp.maximum(mx, s[:, j * lw:(j + 1) * lw])
        mc = jnp.max(mx, axis=-1, keepdims=True) * c
        pm = jnp.exp2(sm * c - mc)
        lsum = pm
        o = jnp.dot(pm.astype(BF16), vm, preferred_element_type=F32)
        for j in range(ns // kc):
            parts = []
            for i in range(kc // lw):
                lo = j * kc + i * lw
                p = jnp.exp2(s[:, lo:lo + lw] * c - mc)
                lsum = lsum + p
                parts.append(p.astype(BF16))
            o = o + jnp.dot(jnp.concatenate(parts, axis=1), v[j * kc:(j + 1) * kc, :],
                            preferred_element_type=F32)
        o_ref[:, sl] = o / jnp.sum(lsum, axis=-1, keepdims=True)


def _attention(q, k, v, km, vm, b, s, tq):
    t, d_attn = q.shape
    wq = Q_PER_KV * HEAD_DIM
    nq = s // tq
    mp = km.shape[0]
    return pl.pallas_call(
        _attn_kernel,
        out_shape=jax.ShapeDtypeStruct((t, d_attn), F32),
        grid=(b, N_KV, nq),
        in_specs=[pl.BlockSpec((tq, wq), lambda bi, kh, qi: (bi * nq + qi, kh)),
                  pl.BlockSpec((s, HEAD_DIM), lambda bi, kh, qi: (bi, kh)),
                  pl.BlockSpec((s, HEAD_DIM), lambda bi, kh, qi: (bi, kh)),
                  pl.BlockSpec((mp, HEAD_DIM), lambda bi, kh, qi: (0, kh)),
                  pl.BlockSpec((mp, HEAD_DIM), lambda bi, kh, qi: (0, kh))],
        out_specs=pl.BlockSpec((tq, wq), lambda bi, kh, qi: (bi * nq + qi, kh)),
        compiler_params=_cparams(("parallel", "parallel", "arbitrary")),
        name="attention",
    )(q, k, v, km, vm)


def _lru_kernel(xr_ref, yr_ref, xm_ref, cw_ref, cb_ref, wg_ref, bg_ref, lam_ref, o_ref,
                xpad, af, bf, ab, bb, *, s, rc):
    l = s + N_META
    bw = LRU_BW
    xpad[0:8, :] = jnp.zeros((8, bw), F32)
    xpad[8:8 + N_META, :] = xm_ref[...]
    xpad[8 + N_META:8 + l, :] = xr_ref[...]
    xpad[8 + l:16 + l, :] = jnp.zeros((8, bw), F32)

    lam = lam_ref[...]
    neg = -lam
    sp = jnp.maximum(neg, 0.0) + jnp.log1p(jnp.exp(-jnp.abs(neg)))
    cw = cw_ref[...]
    cb = cb_ref[...]
    wg = wg_ref[...]
    bg = bg_ref[...]

    def gates(t0, n):
        w = xpad[pl.ds(t0, n + 16), :]
        xc = (cw[0:1] * w[6:6 + n] + cw[1:2] * w[7:7 + n]
              + cw[2:3] * w[8:8 + n] + cw[3:4] * w[9:9 + n]) + cb
        gt = jnp.dot(xc.astype(BF16), wg, preferred_element_type=F32) + bg
        hx = 0.5 * xc
        for d, (a_ref, b_ref) in enumerate(((af, bf), (ab, bb))):
            tr = jnp.tanh(gt[:, (2 * d) * bw:(2 * d + 1) * bw])
            ti = jnp.tanh(gt[:, (2 * d + 1) * bw:(2 * d + 2) * bw])
            hc = (-0.5 * LRU_C) * sp[d:d + 1]
            log_a = hc + hc * tr
            a = jnp.exp(log_a)
            om = 1.0 - a * a
            root = jnp.where(om > 0.0, om * lax.rsqrt(om), 0.0)
            a_ref[pl.ds(t0, n), :] = a
            b_ref[pl.ds(t0, n), :] = root * (hx + hx * ti)

    gates(0, N_META)

    def gate_body(c, carry):
        gates(pl.multiple_of(N_META + c * rc, 8), rc)
        return carry

    lax.fori_loop(0, s // rc, gate_body, 0)

    nch = 8
    cl = l // nch

    def rows(ref, i):
        return ref.at[pl.ds(i, nch, stride=cl), :]

    def pass1(i, carry):
        hf, pf, hb, pb = carry
        a = rows(af, i)[...]
        hf = a * hf + rows(bf, i)[...]
        pf = a * pf
        rows(bf, i)[...] = hf
        rows(af, i)[...] = pf
        j = cl - 1 - i
        a = rows(ab, j)[...]
        hb = a * hb + rows(bb, j)[...]
        pb = a * pb
        rows(bb, j)[...] = hb
        rows(ab, j)[...] = pb
        return hf, pf, hb, pb

    zero = jnp.zeros((nch, bw), F32)
    one = jnp.ones((nch, bw), F32)
    hf, pf, hb, pb = lax.fori_loop(0, cl, pass1, (zero, one, zero, one), unroll=2)

    cf = [jnp.zeros((1, bw), F32)]
    for c in range(1, nch):
        cf.append(hf[c - 1:c] + pf[c - 1:c] * cf[-1])
    cin_f = jnp.concatenate(cf, axis=0)
    cbk = [jnp.zeros((1, bw), F32)]
    for c in range(nch - 2, -1, -1):
        cbk.append(hb[c + 1:c + 2] + pb[c + 1:c + 2] * cbk[-1])
    cin_b = jnp.concatenate(cbk[::-1], axis=0)

    def pass2(i, carry):
        rows(bf, i)[...] = rows(bf, i)[...] + rows(af, i)[...] * cin_f
        rows(bb, i)[...] = rows(bb, i)[...] + rows(ab, i)[...] * cin_b
        return carry

    lax.fori_loop(0, cl, pass2, 0, unroll=2)

    for c in range(s // rc):
        r0 = c * rc
        hsum = bf[N_META + r0:N_META + r0 + rc, :] + bb[N_META + r0:N_META + r0 + rc, :]
        o_ref[r0:r0 + rc, :] = hsum * _gelu(yr_ref[r0:r0 + rc, :])


def _lru(z, zm, cw, cb, wg, bg, lam, b, s, off_x, off_y, d_rnn):
    nblk = d_rnn // LRU_BW
    rc = min(256, s)
    l = s + N_META
    bx = off_x // LRU_BW
    by = off_y // LRU_BW
    return pl.pallas_call(
        functools.partial(_lru_kernel, s=s, rc=rc),
        out_shape=jax.ShapeDtypeStruct((b * s, d_rnn), F32),
        grid=(b, nblk),
        in_specs=[pl.BlockSpec((s, LRU_BW), lambda bi, n: (bi, bx + n)),
                  pl.BlockSpec((s, LRU_BW), lambda bi, n: (bi, by + n)),
                  pl.BlockSpec((N_META, LRU_BW), lambda bi, n: (0, bx + n)),
                  pl.BlockSpec((CONV_W, LRU_BW), lambda bi, n: (0, n)),
                  pl.BlockSpec((1, LRU_BW), lambda bi, n: (0, n)),
                  pl.BlockSpec((None, LRU_BW, 4 * LRU_BW), lambda bi, n: (n, 0, 0)),
                  pl.BlockSpec((None, 1, 4 * LRU_BW), lambda bi, n: (n, 0, 0)),
                  pl.BlockSpec((2, LRU_BW), lambda bi, n: (0, n))],
        out_specs=pl.BlockSpec((s, LRU_BW), lambda bi, n: (bi, n)),
        scratch_shapes=[pltpu.VMEM((l + 16, LRU_BW), F32)] + [pltpu.VMEM((l, LRU_BW), F32)] * 4,
        compiler_params=_cparams(("parallel", "parallel")),
        name="rg_lru",
    )(z, z, zm, cw, cb, wg, bg, lam)


def _out_proj_kernel(a_ref, r_ref, ga_ref, gr_ref, w_ref, x_ref, o_ref, m_ref):
    da = a_ref.shape[1]

    @pl.when(pl.program_id(1) == 0)
    def _():
        m_ref[:, 0:da] = _rms(a_ref[...], ga_ref[...]).astype(BF16)
        m_ref[:, da:] = _rms(r_ref[...], gr_ref[...]).astype(BF16)

    o_ref[...] = x_ref[...] + jnp.dot(m_ref[...], w_ref[...], preferred_element_type=F32)


def _out_proj(attn, lru, ga, gr, w, x2d, tm, tn):
    t, da = attn.shape
    dr = lru.shape[1]
    d = w.shape[1]
    return pl.pallas_call(
        _out_proj_kernel,
        out_shape=jax.ShapeDtypeStruct((t, d), F32),
        grid=(t // tm, d // tn),
        in_specs=[pl.BlockSpec((tm, da), lambda i, j: (i, 0)),
                  pl.BlockSpec((tm, dr), lambda i, j: (i, 0)),
                  pl.BlockSpec((1, da), lambda i, j: (0, 0)),
                  pl.BlockSpec((1, dr), lambda i, j: (0, 0)),
                  pl.BlockSpec((da + dr, tn), lambda i, j: (0, j)),
                  pl.BlockSpec((tm, tn), lambda i, j: (i, j))],
        out_specs=pl.BlockSpec((tm, tn), lambda i, j: (i, j)),
        scratch_shapes=[pltpu.VMEM((tm, da + dr), BF16)],
        compiler_params=_cparams(("parallel", "arbitrary")),
        name="out_proj",
    )(attn, lru, ga, gr, w, x2d)


def _topk_rows(sc, k):
    n, tm = sc.shape
    row = lax.broadcasted_iota(I32, (n, tm), 0)
    cur = sc
    rank = jnp.full((n, tm), k, I32)
    vals = []
    for r in range(k):
        mx = jnp.max(cur, axis=0, keepdims=True)
        idx = jnp.min(jnp.where(cur == mx, row, n), axis=0, keepdims=True)
        sel = row == idx
        rank = jnp.where(sel, r, rank)
        cur = jnp.where(sel, -jnp.inf, cur)
        vals.append(mx)
    return jnp.concatenate(vals, axis=0), rank


def _topk_rows_distinct(sc, k):
    cur = sc
    rank = jnp.full(sc.shape, k, I32)
    vals = []
    for r in range(k):
        mx = jnp.max(cur, axis=0, keepdims=True)
        sel = cur == mx
        rank = jnp.where(sel, r, rank)
        cur = jnp.where(sel, -jnp.inf, cur)
        vals.append(mx)
    return jnp.concatenate(vals, axis=0), rank


def _count_rows(x):
    return jnp.sum(x.astype(I32), axis=0, keepdims=True)


def _route_kernel(h_ref, g_ref, wq_ref, sk_ref, xn_ref, e1_ref, c1_ref, e2_ref, r2_ref,
                  q_scr, r1_s, r2_s, cnt_s, z_s):
    k = PEER_TOPK
    xn = _rms(h_ref[...], g_ref[...]).astype(BF16)
    xn_ref[...] = xn
    q = jnp.dot(xn, wq_ref[...], preferred_element_type=F32).astype(BF16)
    tm = q.shape[0]
    for c in range(2 * PEER_HEADS):
        q_scr[c] = q[:, c * N_KEYS:(c + 1) * N_KEYS]

    hk = k // 2
    pos_col = jnp.concatenate(
        [lax.broadcasted_iota(I32, (k, 1), 0)]
        + [a * k + lax.broadcasted_iota(I32, (hk, 1), 0) for a in range(1, k)], axis=0)
    nc = pos_col.shape[0]
    arow = lax.broadcasted_iota(I32, (k, tm), 0)

    def candidates(v1, v2):
        return jnp.concatenate(
            [v1[0:1] + v2] + [v1[a:a + 1] + v2[0:hk] for a in range(1, k)], axis=0)

    def head_body(h, carry):
        s1 = _dot_nt(sk_ref[2 * h], q_scr[2 * h])
        s2 = _dot_nt(sk_ref[2 * h + 1], q_scr[2 * h + 1])

        v1, r1 = _topk_rows_distinct(s1, k)
        v2, r2 = _topk_rows_distinct(s2, k)
        cand = candidates(v1, v2)
        cur = cand
        top = v1[0:1] + v2[0:1]
        zsum = jnp.zeros((1, tm), F32)
        mx = top
        for j in range(k):
            mx = jnp.max(cur, axis=0, keepdims=True)
            cur = jnp.where(cur == mx, -jnp.inf, cur)
            zsum = zsum + jnp.exp(mx - top)
        selc = cand >= mx
        cnt = jnp.concatenate(
            [_count_rows(selc[0:k])]
            + [_count_rows(selc[k + hk * (a - 1):k + hk * a]) for a in range(1, k)], axis=0)
        bad = jnp.logical_or(
            jnp.logical_or(_count_rows(r1 < k) != k, _count_rows(r2 < k) != k),
            _count_rows(cnt) != k)
        r1_s[...] = r1
        r2_s[...] = r2
        cnt_s[...] = cnt
        z_s[...] = zsum

        @pl.when(jnp.max(bad.astype(I32)) > 0)
        def _():
            v1x, r1x = _topk_rows(s1, k)
            v2x, r2x = _topk_rows(s2, k)
            pos = jnp.broadcast_to(pos_col, (nc, tm))
            curx = candidates(v1x, v2x)
            mask_a = jnp.zeros((k, tm), I32)
            zx = jnp.zeros((1, tm), F32)
            for j in range(k):
                mxx = jnp.max(curx, axis=0, keepdims=True)
                pj = jnp.min(jnp.where(curx == mxx, pos, k * k), axis=0, keepdims=True)
                curx = jnp.where(pos == pj, -jnp.inf, curx)
                zx = zx + jnp.exp(mxx - top)
                aj = jnp.right_shift(pj, 4)
                bj = jnp.bitwise_and(pj, k - 1)
                mask_a = jnp.where(arow == aj, jnp.bitwise_or(mask_a, jnp.left_shift(1, bj)), mask_a)
            r1_s[...] = r1x
            r2_s[...] = r2x
            cnt_s[...] = lax.population_count(mask_a)
            z_s[...] = zx

        r1f = r1_s[...]
        cntf = cnt_s[...]
        c1 = jnp.zeros(r1f.shape, I32)
        for a in range(k):
            c1 = jnp.where(r1f == a, cntf[a:a + 1], c1)
        e1_ref[h] = jnp.exp(s1 - v1[0:1] - jnp.log(z_s[...]))
        c1_ref[h] = c1.astype(F32)
        e2_ref[h] = jnp.exp(s2 - v2[0:1]).astype(BF16)
        r2_ref[h] = r2_s[...].astype(F32).astype(BF16)
        return carry

    lax.fori_loop(0, PEER_HEADS, head_body, 0)


def _route(h1, g, wq, sk, tm):
    t, d = h1.shape
    nq = wq.shape[1]
    rf = jax.ShapeDtypeStruct((PEER_HEADS, N_KEYS, t), F32)
    rb = jax.ShapeDtypeStruct((PEER_HEADS, N_KEYS, t), BF16)
    rspec = pl.BlockSpec((PEER_HEADS, N_KEYS, tm), lambda i: (0, 0, i))
    return pl.pallas_call(
        _route_kernel,
        out_shape=(jax.ShapeDtypeStruct((t, d), BF16), rf, rf, rb, rb),
        grid=(t // tm,),
        in_specs=[pl.BlockSpec((tm, d), lambda i: (i, 0)),
                  pl.BlockSpec((1, d), lambda i: (0, 0)),
                  pl.BlockSpec((d, nq), lambda i: (0, 0), pipeline_mode=pl.Buffered(1)),
                  pl.BlockSpec((2 * PEER_HEADS, N_KEYS, N_KEYS), lambda i: (0, 0, 0))],
        out_specs=(pl.BlockSpec((tm, d), lambda i: (i, 0)), rspec, rspec, rspec, rspec),
        scratch_shapes=[pltpu.VMEM((2 * PEER_HEADS, tm, N_KEYS), BF16),
                        pltpu.VMEM((N_KEYS, tm), I32), pltpu.VMEM((N_KEYS, tm), I32),
                        pltpu.VMEM((PEER_TOPK, tm), I32), pltpu.VMEM((1, tm), F32)],
        compiler_params=_cparams(("parallel",)),
        name="peer_route",
    )(h1, g, wq, sk)


def _gelu_rcp(x):
    k0 = -2.0 * math.sqrt(2.0 / math.pi)
    return x / (1.0 + jnp.exp(x * (k0 + (k0 * 0.044715) * (x * x))))


def _rows_bf16(x):
    t16 = jnp.broadcast_to(x, (16, x.shape[1])).astype(BF16)
    return jnp.concatenate([t16] * (N_KEYS // 16), axis=0)


def _peer_kernel(xn_ref, h_ref, u_ref, v_ref, e1_ref, c1_ref, e2_ref, r2_ref, o_ref, *, nb):
    e = pl.program_id(1)

    @pl.when(e == 0)
    def _():
        o_ref[...] = h_ref[...]

    act = jnp.dot(u_ref[...], xn_ref[...], preferred_element_type=F32)
    ga = _gelu_rcp(act)
    tm = ga.shape[1]
    parts = []
    for kb in range(nb):
        n1 = e * nb + kb
        w = jnp.zeros((N_KEYS, tm), BF16)
        for h in range(PEER_HEADS):
            c1 = _rows_bf16(c1_ref[h, pl.ds(n1, 1), :])
            e1 = _rows_bf16(e1_ref[h, pl.ds(n1, 1), :])
            w = w + jnp.where(r2_ref[h] < c1, e1 * e2_ref[h], jnp.zeros_like(e1))
        parts.append((w.astype(F32) * ga[kb * N_KEYS:(kb + 1) * N_KEYS, :]).T.astype(BF16))
    wa = jnp.concatenate(parts, axis=1)
    o_ref[...] += jnp.dot(wa, v_ref[...], preferred_element_type=F32)


def _peer(xnt, h1, u, v, e1, c1, e2, r2, tm, ec):
    d, t = xnt.shape
    nchunk = u.shape[0] // ec
    nb = ec // N_KEYS
    one = pl.Buffered(1)
    rspec = pl.BlockSpec((PEER_HEADS, N_KEYS, tm), lambda i, e: (0, 0, i), pipeline_mode=one)
    return pl.pallas_call(
        functools.partial(_peer_kernel, nb=nb),
        out_shape=jax.ShapeDtypeStruct((t, d), F32),
        grid=(t // tm, nchunk),
        in_specs=[pl.BlockSpec((d, tm), lambda i, e: (0, i), pipeline_mode=one),
                  pl.BlockSpec((tm, d), lambda i, e: (i, 0), pipeline_mode=one),
                  pl.BlockSpec((ec, d), lambda i, e: (e, 0)),
                  pl.BlockSpec((ec, d), lambda i, e: (e, 0)),
                  rspec, rspec, rspec, rspec],
        out_specs=pl.BlockSpec((tm, d), lambda i, e: (i, 0)),
        compiler_params=_cparams(("parallel", "arbitrary")),
        name="peer_experts",
    )(xnt, h1, u, v, e1, c1, e2, r2)


def _rope_tables(s):
    rows = s // GRID_W
    row = jnp.repeat(jnp.arange(rows, dtype=F32), GRID_W)
    col = (jnp.arange(rows * GRID_W) % GRID_W).astype(F32)
    inv = ROPE_THETA ** (-jnp.arange(ROPE_PAIRS, dtype=F32) / ROPE_PAIRS)
    ar = row[:, None] * inv[None, :]
    ac = col[:, None] * inv[None, :]
    ang = jnp.concatenate([ar, ar, ac, ac], axis=-1)
    lane = jnp.arange(HEAD_DIM)
    sign = jnp.where((lane % (2 * ROPE_PAIRS)) < ROPE_PAIRS, -1.0, 1.0).astype(F32)
    return jnp.cos(ang), jnp.sin(ang) * sign[None, :]


def _tile(n, pref):
    t = min(pref, n)
    while n % t:
        t //= 2
    return t


def kernel(x, meta_tokens, norm1_g, w_in, q_norm_g, k_norm_g, conv_w, conv_b, w_rg, b_rg, w_ig,
           b_ig, lru_lambda, attn_out_g, lru_out_g, w_out, norm2_g, peer_wq, peer_subkeys,
           peer_u, peer_v):
    b, s, d = x.shape
    t = b * s
    depth = w_in.shape[0]
    assert depth == 1, "meta rows of the stream are only materialised as attention / recurrence context"
    d_attn = attn_out_g.shape[-1]
    d_rnn = lru_out_g.shape[-1]
    d_kv = N_KV * HEAD_DIM
    off_x = d_attn + 2 * d_kv
    off_y = off_x + d_rnn
    nblk = d_rnn // LRU_BW

    x2d = x.reshape(t, d)
    meta = meta_tokens.astype(x.dtype)
    g1 = norm1_g[0].reshape(1, d)
    w_in_b = w_in[0].astype(BF16)

    tm = _tile(t, 512)
    z = _in_proj(x2d, g1, w_in_b, tm, 1024)
    zm = _in_proj(meta, g1, w_in_b, N_META, 1024)

    cos, sin_s = _rope_tables(s)
    qg = q_norm_g[0].reshape(1, HEAD_DIM)
    kg = k_norm_g[0].reshape(1, HEAD_DIM)
    q, k, v = _qk_prep(z, cos, sin_s, qg, kg, _tile(s, 256), d_attn)
    _, km, vm = _qk_prep(zm, jnp.ones((N_META, HEAD_DIM), F32), jnp.zeros((N_META, HEAD_DIM), F32),
                         qg, kg, N_META, d_attn)
    pad = ((0, N_KEYS - N_META), (0, 0))
    attn = _attention(q, k, v, jnp.pad(km, pad), jnp.pad(vm, pad), b, s, _tile(s, 256))

    wg = jnp.concatenate([w_rg[0, 0], w_ig[0, 0], w_rg[0, 1], w_ig[0, 1]], axis=-1)
    wg = (0.5 * wg).astype(BF16)
    bg = 0.5 * jnp.stack([b_rg[0, 0], b_ig[0, 0], b_rg[0, 1], b_ig[0, 1]], axis=0)
    bg = bg.reshape(4, nblk, LRU_BW).transpose(1, 0, 2).reshape(nblk, 1, 4 * LRU_BW)
    lru = _lru(z, zm, conv_w[0], conv_b[0].reshape(1, d_rnn), wg, bg, lru_lambda[0],
               b, s, off_x, off_y, d_rnn)

    h1 = _out_proj(attn, lru, attn_out_g[0].reshape(1, d_attn), lru_out_g[0].reshape(1, d_rnn),
                   w_out[0].astype(BF16), x2d, tm, 1024)

    sk = peer_subkeys[0].reshape(2 * PEER_HEADS, N_KEYS, -1).astype(BF16)
    xn2, e1, c1, e2, r2 = _route(h1, norm2_g[0].reshape(1, d), peer_wq[0].astype(BF16), sk,
                                 _tile(t, 256))
    out = _peer(xn2.T, h1, peer_u[0].astype(BF16), peer_v[0].astype(BF16), e1, c1, e2, r2,
                _tile(t, 512), 512)
    return out.reshape(b, s, d)
```

```python
import functools
import math

import jax
import jax.numpy as jnp
from jax import lax
from jax.experimental import pallas as pl
from jax.experimental.pallas import tpu as pltpu

F32 = jnp.float32
BF16 = jnp.bfloat16
I32 = jnp.int32

EPS = 1e-6
N_META = 16
GRID_W = 64
HEAD_DIM = 128
N_KV = 4
Q_PER_KV = 4
ROPE_THETA = 10000.0
ROPE_PAIRS = HEAD_DIM // 4
LRU_BW = 128
LRU_C = 8.0
CONV_W = 4
PEER_HEADS = 8
N_KEYS = 128
PEER_TOPK = 16
ATTN_SCALE = 1.0 / math.sqrt(HEAD_DIM)
NEG_BIG = -1e30

VMEM_LIMIT = 60 * 1024 * 1024


def _cparams(sem, flags=None):
    return pltpu.CompilerParams(dimension_semantics=sem, vmem_limit_bytes=VMEM_LIMIT, flags=flags)


def _rms(x, g):
    ms = jnp.mean(x * x, axis=-1, keepdims=True)
    return x * lax.rsqrt(ms + EPS) * g


def _gelu(x):
    c = math.sqrt(2.0 / math.pi)
    return 0.5 * x * (1.0 + jnp.tanh(c * (x + 0.044715 * (x * x * x))))


def _dot_nt(a, b):
    return lax.dot_general(a, b, (((1,), (1,)), ((), ())), preferred_element_type=F32)


def _in_proj_kernel(x_ref, g_ref, w_ref, o_ref, xn_ref):
    @pl.when(pl.program_id(1) == 0)
    def _():
        xn_ref[...] = _rms(x_ref[...], g_ref[...]).astype(BF16)

    o_ref[...] = jnp.dot(xn_ref[...], w_ref[...], preferred_element_type=F32)


def _in_proj(x2d, g, w, tm, tn):
    t, d = x2d.shape
    n = w.shape[1]
    return pl.pallas_call(
        _in_proj_kernel,
        out_shape=jax.ShapeDtypeStruct((t, n), F32),
        grid=(t // tm, n // tn),
        in_specs=[pl.BlockSpec((tm, d), lambda i, j: (i, 0)),
                  pl.BlockSpec((1, d), lambda i, j: (0, 0)),
                  pl.BlockSpec((d, tn), lambda i, j: (0, j))],
        out_specs=pl.BlockSpec((tm, tn), lambda i, j: (i, j)),
        scratch_shapes=[pltpu.VMEM((tm, d), BF16)],
        compiler_params=_cparams(("parallel", "arbitrary")),
        name="in_proj",
    )(x2d, g, w)


def _qk_prep_kernel(zqk_ref, zv_ref, cos_ref, sin_ref, qg_ref, kg_ref,
                    q_ref, k_ref, v_ref, *, n_q):
    cos = cos_ref[...]
    sin = sin_ref[...]
    lane = lax.broadcasted_iota(I32, cos.shape, 1)
    lo = (lane % (2 * ROPE_PAIRS)) < ROPE_PAIRS

    def norm_rope(t, g):
        tn = _rms(t, g)
        rot = jnp.where(lo, pltpu.roll(tn, HEAD_DIM - ROPE_PAIRS, 1), pltpu.roll(tn, ROPE_PAIRS, 1))
        return (tn * cos + rot * sin).astype(BF16)

    for h in range(n_q):
        sl = slice(h * HEAD_DIM, (h + 1) * HEAD_DIM)
        q_ref[:, sl] = norm_rope(zqk_ref[:, sl], qg_ref[...])
    for h in range(N_KV):
        src = slice((n_q + h) * HEAD_DIM, (n_q + h + 1) * HEAD_DIM)
        k_ref[:, h * HEAD_DIM:(h + 1) * HEAD_DIM] = norm_rope(zqk_ref[:, src], kg_ref[...])
    v_ref[...] = zv_ref[...].astype(BF16)


def _qk_prep(z, cos, sin_signed, qg, kg, tm, d_attn):
    t = z.shape[0]
    n_q = d_attn // HEAD_DIM
    d_kv = N_KV * HEAD_DIM
    w_qk = d_attn + d_kv
    n_tab = cos.shape[0] // tm
    return pl.pallas_call(
        functools.partial(_qk_prep_kernel, n_q=n_q),
        out_shape=(jax.ShapeDtypeStruct((t, d_attn), BF16),
                   jax.ShapeDtypeStruct((t, d_kv), BF16),
                   jax.ShapeDtypeStruct((t, d_kv), BF16)),
        grid=(t // tm,),
        in_specs=[pl.BlockSpec((tm, w_qk), lambda i: (i, 0)),
                  pl.BlockSpec((tm, d_kv), lambda i: (i, w_qk // d_kv)),
                  pl.BlockSpec((tm, HEAD_DIM), lambda i: (i % n_tab, 0)),
                  pl.BlockSpec((tm, HEAD_DIM), lambda i: (i % n_tab, 0)),
                  pl.BlockSpec((1, HEAD_DIM), lambda i: (0, 0)),
                  pl.BlockSpec((1, HEAD_DIM), lambda i: (0, 0))],
        out_specs=(pl.BlockSpec((tm, d_attn), lambda i: (i, 0)),
                   pl.BlockSpec((tm, d_kv), lambda i: (i, 0)),
                   pl.BlockSpec((tm, d_kv), lambda i: (i, 0))),
        compiler_params=_cparams(("parallel",)),
        name="qk_prep",
    )(z, z, cos, sin_signed, qg, kg)


def _attn_head(q, k, v, km, vm, meta_bias, mc):
    c = ATTN_SCALE * math.log2(math.e)
    ns = k.shape[0]
    kc = min(ns, 512)
    lw = km.shape[0]
    sm = _dot_nt(q, km) + meta_bias
    if mc is None:
        s = _dot_nt(q, k)
        chunks = [s[:, j * kc:(j + 1) * kc] for j in range(ns // kc)]
        mx = sm
        for j in range(ns // lw):
            mx = jnp.maximum(mx, s[:, j * lw:(j + 1) * lw])
        mc = jnp.max(mx, axis=-1, keepdims=True) * c
    else:
        chunks = [_dot_nt(q, k[j * kc:(j + 1) * kc, :]) for j in range(ns // kc)]
    pm = jnp.exp2(sm * c - mc)
    lsum = pm
    o = jnp.dot(pm.astype(BF16), vm, preferred_element_type=F32)
    for j in range(ns // kc):
        parts = []
        for i in range(kc // lw):
            p = jnp.exp2(chunks[j][:, i * lw:(i + 1) * lw] * c - mc)
            lsum = lsum + p
            parts.append(p.astype(BF16))
        o = o + jnp.dot(jnp.concatenate(parts, axis=1), v[j * kc:(j + 1) * kc, :],
                        preferred_element_type=F32)
    return o, jnp.sum(lsum, axis=-1, keepdims=True)


ATTN_MIN_ROW_SUM = 2.0 ** -60


def _attn_kernel(q_ref, k_ref, v_ref, km_ref, vm_ref, o_ref, kmax_scr):
    k = k_ref[...]
    v = v_ref[...]
    km = km_ref[...]
    vm = vm_ref[...]
    col = lax.broadcasted_iota(I32, (1, km.shape[0]), 1)
    meta_bias = jnp.where(col < N_META, 0.0, NEG_BIG).astype(F32)
    c = ATTN_SCALE * math.log2(math.e)

    @pl.when(pl.program_id(2) == 0)
    def _():
        kf = k.astype(F32)
        kmf = km.astype(F32)
        n2 = jnp.maximum(jnp.max(jnp.sum(kf * kf, axis=-1, keepdims=True), axis=0, keepdims=True),
                         jnp.max(jnp.sum(kmf * kmf, axis=-1, keepdims=True), axis=0, keepdims=True))
        kmax_scr[...] = jnp.broadcast_to(jnp.sqrt(n2), kmax_scr.shape)

    kmax_c = kmax_scr[:, 0:1] * c
    lmin = None
    for g in range(Q_PER_KV):
        sl = slice(g * HEAD_DIM, (g + 1) * HEAD_DIM)
        q = q_ref[:, sl]
        qf = q.astype(F32)
        mc = jnp.sqrt(jnp.sum(qf * qf, axis=-1, keepdims=True)) * kmax_c
        o, l = _attn_head(q, k, v, km, vm, meta_bias, mc)
        o_ref[:, sl] = o / l
        lg = jnp.min(l)
        lmin = lg if lmin is None else jnp.minimum(lmin, lg)

    @pl.when(jnp.logical_not(lmin >= ATTN_MIN_ROW_SUM))
    def _():
        for g in range(Q_PER_KV):
            sl = slice(g * HEAD_DIM, (g + 1) * HEAD_DIM)
            o, l = _attn_head(q_ref[:, sl], k, v, km, vm, meta_bias, None)
            o_ref[:, sl] = o / l


def _attention(q, k, v, km, vm, b, s, tq):
    t, d_attn = q.shape
    wq = Q_PER_KV * HEAD_DIM
    nq = s // tq
    mp = km.shape[0]
    return pl.pallas_call(
        _attn_kernel,
        out_shape=jax.ShapeDtypeStruct((t, d_attn), F32),
        grid=(b, N_KV, nq),
        in_specs=[pl.BlockSpec((tq, wq), lambda bi, kh, qi: (bi * nq + qi, kh)),
                  pl.BlockSpec((s, HEAD_DIM), lambda bi, kh, qi: (bi, kh)),
                  pl.BlockSpec((s, HEAD_DIM), lambda bi, kh, qi: (bi, kh)),
                  pl.BlockSpec((mp, HEAD_DIM), lambda bi, kh, qi: (0, kh)),
                  pl.BlockSpec((mp, HEAD_DIM), lambda bi, kh, qi: (0, kh))],
        out_specs=pl.BlockSpec((tq, wq), lambda bi, kh, qi: (bi * nq + qi, kh)),
        scratch_shapes=[pltpu.VMEM((1, HEAD_DIM), F32)],
        compiler_params=_cparams(("parallel", "parallel", "arbitrary")),
        name="attention",
    )(q, k, v, km, vm)


def _lru_kernel(xr_ref, yr_ref, xm_ref, cw_ref, cb_ref, wg_ref, bg_ref, lam_ref, o_ref,
                xpad, af, bf, ab, bb, *, s, rc):
    l = s + N_META
    bw = LRU_BW
    xpad[0:8, :] = jnp.zeros((8, bw), F32)
    xpad[8:8 + N_META, :] = xm_ref[...]
    xpad[8 + N_META:8 + l, :] = xr_ref[...]
    xpad[8 + l:16 + l, :] = jnp.zeros((8, bw), F32)

    lam = lam_ref[...]
    neg = -lam
    sp = jnp.maximum(neg, 0.0) + jnp.log1p(jnp.exp(-jnp.abs(neg)))
    cw = cw_ref[...]
    cb = cb_ref[...]
    wg = wg_ref[...]
    bg = bg_ref[...]

    def gates(t0, n):
        w = xpad[pl.ds(t0, n + 16), :]
        xc = (cw[0:1] * w[6:6 + n] + cw[1:2] * w[7:7 + n]
              + cw[2:3] * w[8:8 + n] + cw[3:4] * w[9:9 + n]) + cb
        gt = jnp.dot(xc.astype(BF16), wg, preferred_element_type=F32) + bg
        hx = 0.5 * xc
        for d, (a_ref, b_ref) in enumerate(((af, bf), (ab, bb))):
            tr = jnp.tanh(gt[:, (2 * d) * bw:(2 * d + 1) * bw])
            ti = jnp.tanh(gt[:, (2 * d + 1) * bw:(2 * d + 2) * bw])
            hc = (-0.5 * LRU_C) * sp[d:d + 1]
            log_a = hc + hc * tr
            a = jnp.exp(log_a)
            om = 1.0 - a * a
            root = jnp.where(om > 0.0, om * lax.rsqrt(om), 0.0)
            a_ref[pl.ds(t0, n), :] = a
            b_ref[pl.ds(t0, n), :] = root * (hx + hx * ti)

    gates(0, N_META)

    def gate_body(c, carry):
        gates(pl.multiple_of(N_META + c * rc, 8), rc)
        return carry

    lax.fori_loop(0, s // rc, gate_body, 0)

    nch = 8
    cl = l // nch

    def rows(ref, i):
        return ref.at[pl.ds(i, nch, stride=cl), :]

    def pass1(i, carry):
        hf, pf, hb, pb = carry
        a = rows(af, i)[...]
        hf = a * hf + rows(bf, i)[...]
        pf = a * pf
        rows(bf, i)[...] = hf
        rows(af, i)[...] = pf
        j = cl - 1 - i
        a = rows(ab, j)[...]
        hb = a * hb + rows(bb, j)[...]
        pb = a * pb
        rows(bb, j)[...] = hb
        rows(ab, j)[...] = pb
        return hf, pf, hb, pb

    zero = jnp.zeros((nch, bw), F32)
    one = jnp.ones((nch, bw), F32)
    hf, pf, hb, pb = lax.fori_loop(0, cl, pass1, (zero, one, zero, one), unroll=2)

    cf = [jnp.zeros((1, bw), F32)]
    for c in range(1, nch):
        cf.append(hf[c - 1:c] + pf[c - 1:c] * cf[-1])
    cin_f = jnp.concatenate(cf, axis=0)
    cbk = [jnp.zeros((1, bw), F32)]
    for c in range(nch - 2, -1, -1):
        cbk.append(hb[c + 1:c + 2] + pb[c + 1:c + 2] * cbk[-1])
    cin_b = jnp.concatenate(cbk[::-1], axis=0)

    def pass2(i, carry):
        rows(bf, i)[...] = rows(bf, i)[...] + rows(af, i)[...] * cin_f
        rows(bb, i)[...] = rows(bb, i)[...] + rows(ab, i)[...] * cin_b
        return carry

    lax.fori_loop(0, cl, pass2, 0, unroll=2)

    for c in range(s // rc):
        r0 = c * rc
        hsum = bf[N_META + r0:N_META + r0 + rc, :] + bb[N_META + r0:N_META + r0 + rc, :]
        o_ref[r0:r0 + rc, :] = hsum * _gelu(yr_ref[r0:r0 + rc, :])


def _lru(z, zm, cw, cb, wg, bg, lam, b, s, off_x, off_y, d_rnn):
    nblk = d_rnn // LRU_BW
    rc = min(256, s)
    l = s + N_META
    bx = off_x // LRU_BW
    by = off_y // LRU_BW
    return pl.pallas_call(
        functools.partial(_lru_kernel, s=s, rc=rc),
        out_shape=jax.ShapeDtypeStruct((b * s, d_rnn), F32),
        grid=(b, nblk),
        in_specs=[pl.BlockSpec((s, LRU_BW), lambda bi, n: (bi, bx + n)),
                  pl.BlockSpec((s, LRU_BW), lambda bi, n: (bi, by + n)),
                  pl.BlockSpec((N_META, LRU_BW), lambda bi, n: (0, bx + n)),
                  pl.BlockSpec((CONV_W, LRU_BW), lambda bi, n: (0, n)),
                  pl.BlockSpec((1, LRU_BW), lambda bi, n: (0, n)),
                  pl.BlockSpec((None, LRU_BW, 4 * LRU_BW), lambda bi, n: (n, 0, 0)),
                  pl.BlockSpec((None, 1, 4 * LRU_BW), lambda bi, n: (n, 0, 0)),
                  pl.BlockSpec((2, LRU_BW), lambda bi, n: (0, n))],
        out_specs=pl.BlockSpec((s, LRU_BW), lambda bi, n: (bi, n)),
        scratch_shapes=[pltpu.VMEM((l + 16, LRU_BW), F32)] + [pltpu.VMEM((l, LRU_BW), F32)] * 4,
        compiler_params=_cparams(("parallel", "parallel")),
        name="rg_lru",
    )(z, z, zm, cw, cb, wg, bg, lam)


def _out_proj_kernel(a_ref, r_ref, ga_ref, gr_ref, w_ref, x_ref, o_ref, m_ref):
    da = a_ref.shape[1]

    @pl.when(pl.program_id(1) == 0)
    def _():
        m_ref[:, 0:da] = _rms(a_ref[...], ga_ref[...]).astype(BF16)
        m_ref[:, da:] = _rms(r_ref[...], gr_ref[...]).astype(BF16)

    o_ref[...] = x_ref[...] + jnp.dot(m_ref[...], w_ref[...], preferred_element_type=F32)


def _out_proj(attn, lru, ga, gr, w, x2d, tm, tn):
    t, da = attn.shape
    dr = lru.shape[1]
    d = w.shape[1]
    return pl.pallas_call(
        _out_proj_kernel,
        out_shape=jax.ShapeDtypeStruct((t, d), F32),
        grid=(t // tm, d // tn),
        in_specs=[pl.BlockSpec((tm, da), lambda i, j: (i, 0)),
                  pl.BlockSpec((tm, dr), lambda i, j: (i, 0)),
                  pl.BlockSpec((1, da), lambda i, j: (0, 0)),
                  pl.BlockSpec((1, dr), lambda i, j: (0, 0)),
                  pl.BlockSpec((da + dr, tn), lambda i, j: (0, j)),
                  pl.BlockSpec((tm, tn), lambda i, j: (i, j))],
        out_specs=pl.BlockSpec((tm, tn), lambda i, j: (i, j)),
        scratch_shapes=[pltpu.VMEM((tm, da + dr), BF16)],
        compiler_params=_cparams(("parallel", "arbitrary")),
        name="out_proj",
    )(attn, lru, ga, gr, w, x2d)


def _topk_rows(sc, k):
    n, tm = sc.shape
    row = lax.broadcasted_iota(I32, (n, tm), 0)
    cur = sc
    rank = jnp.full((n, tm), k, I32)
    vals = []
    for r in range(k):
        mx = jnp.max(cur, axis=0, keepdims=True)
        idx = jnp.min(jnp.where(cur == mx, row, n), axis=0, keepdims=True)
        sel = row == idx
        rank = jnp.where(sel, r, rank)
        cur = jnp.where(sel, -jnp.inf, cur)
        vals.append(mx)
    return jnp.concatenate(vals, axis=0), rank


def _topk_rows_distinct(sc, k):
    cur = sc
    rank = jnp.full(sc.shape, k, I32)
    vals = []
    for r in range(k):
        mx = jnp.max(cur, axis=0, keepdims=True)
        sel = cur == mx
        rank = jnp.where(sel, r, rank)
        cur = jnp.where(sel, -jnp.inf, cur)
        vals.append(mx)
    return jnp.concatenate(vals, axis=0), rank


def _count_rows(x):
    return jnp.sum(x.astype(I32), axis=0, keepdims=True)


def _route_kernel(h_ref, g_ref, wq_ref, sk_ref, xn_ref, e1_ref, c1_ref, e2_ref, r2_ref,
                  q_scr, r1_s, r2_s, cnt_s, z_s):
    k = PEER_TOPK
    xn = _rms(h_ref[...], g_ref[...]).astype(BF16)
    xn_ref[...] = xn
    q = jnp.dot(xn, wq_ref[...], preferred_element_type=F32).astype(BF16)
    tm = q.shape[0]
    for c in range(2 * PEER_HEADS):
        q_scr[c] = q[:, c * N_KEYS:(c + 1) * N_KEYS]

    hk = k // 2
    pos_col = jnp.concatenate(
        [lax.broadcasted_iota(I32, (k, 1), 0)]
        + [a * k + lax.broadcasted_iota(I32, (hk, 1), 0) for a in range(1, k)], axis=0)
    nc = pos_col.shape[0]
    arow = lax.broadcasted_iota(I32, (k, tm), 0)

    def candidates(v1, v2):
        return jnp.concatenate(
            [v1[0:1] + v2] + [v1[a:a + 1] + v2[0:hk] for a in range(1, k)], axis=0)

    def head_body(h, carry):
        s1 = _dot_nt(sk_ref[2 * h], q_scr[2 * h])
        s2 = _dot_nt(sk_ref[2 * h + 1], q_scr[2 * h + 1])

        v1, r1 = _topk_rows_distinct(s1, k)
        v2, r2 = _topk_rows_distinct(s2, k)
        cand = candidates(v1, v2)
        cur = cand
        top = v1[0:1] + v2[0:1]
        zsum = jnp.zeros((1, tm), F32)
        mx = top
        for j in range(k):
            mx = jnp.max(cur, axis=0, keepdims=True)
            cur = jnp.where(cur == mx, -jnp.inf, cur)
            zsum = zsum + jnp.exp(mx - top)
        selc = cand >= mx
        cnt = jnp.concatenate(
            [_count_rows(selc[0:k])]
            + [_count_rows(selc[k + hk * (a - 1):k + hk * a]) for a in range(1, k)], axis=0)
        bad = jnp.logical_or(
            jnp.logical_or(_count_rows(r1 < k) != k, _count_rows(r2 < k) != k),
            _count_rows(cnt) != k)
        r1_s[...] = r1
        r2_s[...] = r2
        cnt_s[...] = cnt
        z_s[...] = zsum

        @pl.when(jnp.max(bad.astype(I32)) > 0)
        def _():
            v1x, r1x = _topk_rows(s1, k)
            v2x, r2x = _topk_rows(s2, k)
            pos = jnp.broadcast_to(pos_col, (nc, tm))
            curx = candidates(v1x, v2x)
            mask_a = jnp.zeros((k, tm), I32)
            zx = jnp.zeros((1, tm), F32)
            for j in range(k):
                mxx = jnp.max(curx, axis=0, keepdims=True)
                pj = jnp.min(jnp.where(curx == mxx, pos, k * k), axis=0, keepdims=True)
                curx = jnp.where(pos == pj, -jnp.inf, curx)
                zx = zx + jnp.exp(mxx - top)
                aj = jnp.right_shift(pj, 4)
                bj = jnp.bitwise_and(pj, k - 1)
                mask_a = jnp.where(arow == aj, jnp.bitwise_or(mask_a, jnp.left_shift(1, bj)), mask_a)
            r1_s[...] = r1x
            r2_s[...] = r2x
            cnt_s[...] = lax.population_count(mask_a)
            z_s[...] = zx

        r1f = r1_s[...]
        cntf = cnt_s[...]
        c1 = jnp.zeros(r1f.shape, I32)
        for a in range(k):
            c1 = jnp.where(r1f == a, cntf[a:a + 1], c1)
        e1_ref[h] = jnp.exp(s1 - v1[0:1] - jnp.log(z_s[...]))
        c1_ref[h] = c1.astype(F32)
        e2_ref[h] = jnp.exp(s2 - v2[0:1]).astype(BF16)
        r2_ref[h] = r2_s[...].astype(F32).astype(BF16)
        return carry

    lax.fori_loop(0, PEER_HEADS, head_body, 0)


def _route(h1, g, wq, sk, tm):
    t, d = h1.shape
    nq = wq.shape[1]
    rf = jax.ShapeDtypeStruct((PEER_HEADS, N_KEYS, t), F32)
    rb = jax.ShapeDtypeStruct((PEER_HEADS, N_KEYS, t), BF16)
    rspec = pl.BlockSpec((PEER_HEADS, N_KEYS, tm), lambda i: (0, 0, i))
    return pl.pallas_call(
        _route_kernel,
        out_shape=(jax.ShapeDtypeStruct((t, d), BF16), rf, rf, rb, rb),
        grid=(t // tm,),
        in_specs=[pl.BlockSpec((tm, d), lambda i: (i, 0)),
                  pl.BlockSpec((1, d), lambda i: (0, 0)),
                  pl.BlockSpec((d, nq), lambda i: (0, 0), pipeline_mode=pl.Buffered(1)),
                  pl.BlockSpec((2 * PEER_HEADS, N_KEYS, N_KEYS), lambda i: (0, 0, 0))],
        out_specs=(pl.BlockSpec((tm, d), lambda i: (i, 0)), rspec, rspec, rspec, rspec),
        scratch_shapes=[pltpu.VMEM((2 * PEER_HEADS, tm, N_KEYS), BF16),
                        pltpu.VMEM((N_KEYS, tm), I32), pltpu.VMEM((N_KEYS, tm), I32),
                        pltpu.VMEM((PEER_TOPK, tm), I32), pltpu.VMEM((1, tm), F32)],
        compiler_params=_cparams(("parallel",)),
        name="peer_route",
    )(h1, g, wq, sk)


def _gelu_rcp(x):
    k0 = -2.0 * math.sqrt(2.0 / math.pi)
    return x / (1.0 + jnp.exp(x * (k0 + (k0 * 0.044715) * (x * x))))


def _rows_bf16(x):
    t16 = jnp.broadcast_to(x, (16, x.shape[1])).astype(BF16)
    return jnp.concatenate([t16] * (N_KEYS // 16), axis=0)


def _peer_kernel(xn_ref, h_ref, u_ref, v_ref, e1_ref, c1_ref, e2_ref, r2_ref, o_ref, *, nb):
    e = pl.program_id(1)

    @pl.when(e == 0)
    def _():
        o_ref[...] = h_ref[...]

    act = jnp.dot(u_ref[...], xn_ref[...], preferred_element_type=F32)
    ga = _gelu_rcp(act)
    tm = ga.shape[1]
    parts = []
    for kb in range(nb):
        n1 = e * nb + kb
        w = jnp.zeros((N_KEYS, tm), BF16)
        for h in range(PEER_HEADS):
            c1 = _rows_bf16(c1_ref[h, pl.ds(n1, 1), :])
            e1 = _rows_bf16(e1_ref[h, pl.ds(n1, 1), :])
            w = w + jnp.where(r2_ref[h] < c1, e1 * e2_ref[h], jnp.zeros_like(e1))
        parts.append((w.astype(F32) * ga[kb * N_KEYS:(kb + 1) * N_KEYS, :]).T.astype(BF16))
    wa = jnp.concatenate(parts, axis=1)
    o_ref[...] += jnp.dot(wa, v_ref[...], preferred_element_type=F32)


def _peer(xnt, h1, u, v, e1, c1, e2, r2, tm, ec):
    d, t = xnt.shape
    nchunk = u.shape[0] // ec
    nb = ec // N_KEYS
    one = pl.Buffered(1)
    rspec = pl.BlockSpec((PEER_HEADS, N_KEYS, tm), lambda i, e: (0, 0, i), pipeline_mode=one)
    return pl.pallas_call(
        functools.partial(_peer_kernel, nb=nb),
        out_shape=jax.ShapeDtypeStruct((t, d), F32),
        grid=(t // tm, nchunk),
        in_specs=[pl.BlockSpec((d, tm), lambda i, e: (0, i), pipeline_mode=one),
                  pl.BlockSpec((tm, d), lambda i, e: (i, 0), pipeline_mode=one),
                  pl.BlockSpec((ec, d), lambda i, e: (e, 0)),
                  pl.BlockSpec((ec, d), lambda i, e: (e, 0)),
                  rspec, rspec, rspec, rspec],
        out_specs=pl.BlockSpec((tm, d), lambda i, e: (i, 0)),
        compiler_params=_cparams(("parallel", "arbitrary")),
        name="peer_experts",
    )(xnt, h1, u, v, e1, c1, e2, r2)


def _rope_tables(s):
    rows = s // GRID_W
    row = jnp.repeat(jnp.arange(rows, dtype=F32), GRID_W)
    col = (jnp.arange(rows * GRID_W) % GRID_W).astype(F32)
    inv = ROPE_THETA ** (-jnp.arange(ROPE_PAIRS, dtype=F32) / ROPE_PAIRS)
    ar = row[:, None] * inv[None, :]
    ac = col[:, None] * inv[None, :]
    ang = jnp.concatenate([ar, ar, ac, ac], axis=-1)
    lane = jnp.arange(HEAD_DIM)
    sign = jnp.where((lane % (2 * ROPE_PAIRS)) < ROPE_PAIRS, -1.0, 1.0).astype(F32)
    return jnp.cos(ang), jnp.sin(ang) * sign[None, :]


def _tile(n, pref):
    t = min(pref, n)
    while n % t:
        t //= 2
    return t


def kernel(x, meta_tokens, norm1_g, w_in, q_norm_g, k_norm_g, conv_w, conv_b, w_rg, b_rg, w_ig,
           b_ig, lru_lambda, attn_out_g, lru_out_g, w_out, norm2_g, peer_wq, peer_subkeys,
           peer_u, peer_v):
    b, s, d = x.shape
    t = b * s
    depth = w_in.shape[0]
    assert depth == 1, "meta rows of the stream are only materialised as attention / recurrence context"
    d_attn = attn_out_g.shape[-1]
    d_rnn = lru_out_g.shape[-1]
    d_kv = N_KV * HEAD_DIM
    off_x = d_attn + 2 * d_kv
    off_y = off_x + d_rnn
    nblk = d_rnn // LRU_BW

    x2d = x.reshape(t, d)
    meta = meta_tokens.astype(x.dtype)
    g1 = norm1_g[0].reshape(1, d)
    w_in_b = w_in[0].astype(BF16)

    tm = _tile(t, 512)
    z = _in_proj(x2d, g1, w_in_b, tm, 1024)
    zm = _in_proj(meta, g1, w_in_b, N_META, 1024)

    cos, sin_s = _rope_tables(s)
    qg = q_norm_g[0].reshape(1, HEAD_DIM)
    kg = k_norm_g[0].reshape(1, HEAD_DIM)
    q, k, v = _qk_prep(z, cos, sin_s, qg, kg, _tile(s, 256), d_attn)
    _, km, vm = _qk_prep(zm, jnp.ones((N_META, HEAD_DIM), F32), jnp.zeros((N_META, HEAD_DIM), F32),
                         qg, kg, N_META, d_attn)
    pad = ((0, N_KEYS - N_META), (0, 0))
    attn = _attention(q, k, v, jnp.pad(km, pad), jnp.pad(vm, pad), b, s, _tile(s, 256))

    wg = jnp.concatenate([w_rg[0, 0], w_ig[0, 0], w_rg[0, 1], w_ig[0, 1]], axis=-1)
    wg = (0.5 * wg).astype(BF16)
    bg = 0.5 * jnp.stack([b_rg[0, 0], b_ig[0, 0], b_rg[0, 1], b_ig[0, 1]], axis=0)
    bg = bg.reshape(4, nblk, LRU_BW).transpose(1, 0, 2).reshape(nblk, 1, 4 * LRU_BW)
    lru = _lru(z, zm, conv_w[0], conv_b[0].reshape(1, d_rnn), wg, bg, lru_lambda[0],
               b, s, off_x, off_y, d_rnn)

    h1 = _out_proj(attn, lru, attn_out_g[0].reshape(1, d_attn), lru_out_g[0].reshape(1, d_rnn),
                   w_out[0].astype(BF16), x2d, tm, 1024)

    sk = peer_subkeys[0].reshape(2 * PEER_HEADS, N_KEYS, -1).astype(BF16)
    xn2, e1, c1, e2, r2 = _route(h1, norm2_g[0].reshape(1, d), peer_wq[0].astype(BF16), sk,
                                 _tile(t, 256))
    out = _peer(xn2.T, h1, peer_u[0].astype(BF16), peer_v[0].astype(BF16), e1, c1, e2, r2,
                _tile(t, 512), 512)
    return out.reshape(b, s, d)
```

```python
import functools
import math

import jax
import jax.numpy as jnp
from jax import lax
from jax.experimental import pallas as pl
from jax.experimental.pallas import tpu as pltpu

F32 = jnp.float32
BF16 = jnp.bfloat16
I32 = jnp.int32

EPS = 1e-6
N_META = 16
GRID_W = 64
HEAD_DIM = 128
N_KV = 4
Q_PER_KV = 4
ROPE_THETA = 10000.0
ROPE_PAIRS = HEAD_DIM // 4
LRU_BW = 128
LRU_C = 8.0
CONV_W = 4
PEER_HEADS = 8
N_KEYS = 128
PEER_TOPK = 16
ATTN_SCALE = 1.0 / math.sqrt(HEAD_DIM)
NEG_BIG = -1e30

VMEM_LIMIT = 60 * 1024 * 1024


def _cparams(sem, flags=None):
    return pltpu.CompilerParams(dimension_semantics=sem, vmem_limit_bytes=VMEM_LIMIT, flags=flags)


def _rms(x, g):
    ms = jnp.mean(x * x, axis=-1, keepdims=True)
    return x * lax.rsqrt(ms + EPS) * g


def _gelu(x):
    c = math.sqrt(2.0 / math.pi)
    return 0.5 * x * (1.0 + jnp.tanh(c * (x + 0.044715 * (x * x * x))))


def _dot_nt(a, b):
    return lax.dot_general(a, b, (((1,), (1,)), ((), ())), preferred_element_type=F32)


def _in_proj_kernel(x_ref, g_ref, w_ref, o_ref, xn_ref):
    @pl.when(pl.program_id(1) == 0)
    def _():
        xn_ref[...] = _rms(x_ref[...], g_ref[...]).astype(BF16)

    o_ref[...] = jnp.dot(xn_ref[...], w_ref[...], preferred_element_type=F32)


def _in_proj(x2d, g, w, tm, tn):
    t, d = x2d.shape
    n = w.shape[1]
    return pl.pallas_call(
        _in_proj_kernel,
        out_shape=jax.ShapeDtypeStruct((t, n), F32),
        grid=(t // tm, n // tn),
        in_specs=[pl.BlockSpec((tm, d), lambda i, j: (i, 0)),
                  pl.BlockSpec((1, d), lambda i, j: (0, 0)),
                  pl.BlockSpec((d, tn), lambda i, j: (0, j))],
        out_specs=pl.BlockSpec((tm, tn), lambda i, j: (i, j)),
        scratch_shapes=[pltpu.VMEM((tm, d), BF16)],
        compiler_params=_cparams(("parallel", "arbitrary")),
        name="in_proj",
    )(x2d, g, w)


def _qk_prep_kernel(zqk_ref, zv_ref, cos_ref, sin_ref, qg_ref, kg_ref,
                    q_ref, k_ref, v_ref, *, n_q):
    cos = cos_ref[...]
    sin = sin_ref[...]
    lane = lax.broadcasted_iota(I32, cos.shape, 1)
    lo = (lane % (2 * ROPE_PAIRS)) < ROPE_PAIRS

    def norm_rope(t, g):
        tn = _rms(t, g)
        rot = jnp.where(lo, pltpu.roll(tn, HEAD_DIM - ROPE_PAIRS, 1), pltpu.roll(tn, ROPE_PAIRS, 1))
        return (tn * cos + rot * sin).astype(BF16)

    for h in range(n_q):
        sl = slice(h * HEAD_DIM, (h + 1) * HEAD_DIM)
        q_ref[:, sl] = norm_rope(zqk_ref[:, sl], qg_ref[...])
    for h in range(N_KV):
        src = slice((n_q + h) * HEAD_DIM, (n_q + h + 1) * HEAD_DIM)
        k_ref[:, h * HEAD_DIM:(h + 1) * HEAD_DIM] = norm_rope(zqk_ref[:, src], kg_ref[...])
    v_ref[...] = zv_ref[...].astype(BF16)


def _qk_prep(z, cos, sin_signed, qg, kg, tm, d_attn):
    t = z.shape[0]
    n_q = d_attn // HEAD_DIM
    d_kv = N_KV * HEAD_DIM
    w_qk = d_attn + d_kv
    n_tab = cos.shape[0] // tm
    return pl.pallas_call(
        functools.partial(_qk_prep_kernel, n_q=n_q),
        out_shape=(jax.ShapeDtypeStruct((t, d_attn), BF16),
                   jax.ShapeDtypeStruct((t, d_kv), BF16),
                   jax.ShapeDtypeStruct((t, d_kv), BF16)),
        grid=(t // tm,),
        in_specs=[pl.BlockSpec((tm, w_qk), lambda i: (i, 0)),
                  pl.BlockSpec((tm, d_kv), lambda i: (i, w_qk // d_kv)),
                  pl.BlockSpec((tm, HEAD_DIM), lambda i: (i % n_tab, 0)),
                  pl.BlockSpec((tm, HEAD_DIM), lambda i: (i % n_tab, 0)),
                  pl.BlockSpec((1, HEAD_DIM), lambda i: (0, 0)),
                  pl.BlockSpec((1, HEAD_DIM), lambda i: (0, 0))],
        out_specs=(pl.BlockSpec((tm, d_attn), lambda i: (i, 0)),
                   pl.BlockSpec((tm, d_kv), lambda i: (i, 0)),
                   pl.BlockSpec((tm, d_kv), lambda i: (i, 0))),
        compiler_params=_cparams(("parallel",)),
        name="qk_prep",
    )(z, z, cos, sin_signed, qg, kg)


def _attn_head(q, k, v, km, vm, meta_bias, mc):
    c = ATTN_SCALE * math.log2(math.e)
    ns = k.shape[0]
    kc = min(ns, 512)
    lw = km.shape[0]
    sm = _dot_nt(q, km) + meta_bias
    if mc is None:
        s = _dot_nt(q, k)
        chunks = [s[:, j * kc:(j + 1) * kc] for j in range(ns // kc)]
        mx = sm
        for j in range(ns // lw):
            mx = jnp.maximum(mx, s[:, j * lw:(j + 1) * lw])
        mc = jnp.max(mx, axis=-1, keepdims=True) * c
    else:
        chunks = [_dot_nt(q, k[j * kc:(j + 1) * kc, :]) for j in range(ns // kc)]
    pm = jnp.exp2(sm * c - mc)
    lsum = pm
    o = jnp.dot(pm.astype(BF16), vm, preferred_element_type=F32)
    for j in range(ns // kc):
        parts = []
        for i in range(kc // lw):
            p = jnp.exp2(chunks[j][:, i * lw:(i + 1) * lw] * c - mc)
            lsum = lsum + p
            parts.append(p.astype(BF16))
        o = o + jnp.dot(jnp.concatenate(parts, axis=1), v[j * kc:(j + 1) * kc, :],
                        preferred_element_type=F32)
    return o, jnp.sum(lsum, axis=-1, keepdims=True)


ATTN_MIN_ROW_SUM = 2.0 ** -60


def _attn_kernel(q_ref, k_ref, v_ref, km_ref, vm_ref, o_ref, kmax_scr):
    k = k_ref[...]
    v = v_ref[...]
    km = km_ref[...]
    vm = vm_ref[...]
    col = lax.broadcasted_iota(I32, (1, km.shape[0]), 1)
    meta_bias = jnp.where(col < N_META, 0.0, NEG_BIG).astype(F32)
    c = ATTN_SCALE * math.log2(math.e)

    @pl.when(pl.program_id(2) == 0)
    def _():
        kf = k.astype(F32)
        kmf = km.astype(F32)
        n2 = jnp.maximum(jnp.max(jnp.sum(kf * kf, axis=-1, keepdims=True), axis=0, keepdims=True),
                         jnp.max(jnp.sum(kmf * kmf, axis=-1, keepdims=True), axis=0, keepdims=True))
        kmax_scr[...] = jnp.broadcast_to(jnp.sqrt(n2), kmax_scr.shape)

    kmax_c = kmax_scr[:, 0:1] * c
    lmin = None
    for g in range(Q_PER_KV):
        sl = slice(g * HEAD_DIM, (g + 1) * HEAD_DIM)
        q = q_ref[:, sl]
        qf = q.astype(F32)
        mc = jnp.sqrt(jnp.sum(qf * qf, axis=-1, keepdims=True)) * kmax_c
        o, l = _attn_head(q, k, v, km, vm, meta_bias, mc)
        o_ref[:, sl] = o / l
        lg = jnp.min(l)
        lmin = lg if lmin is None else jnp.minimum(lmin, lg)

    @pl.when(jnp.logical_not(lmin >= ATTN_MIN_ROW_SUM))
    def _():
        for g in range(Q_PER_KV):
            sl = slice(g * HEAD_DIM, (g + 1) * HEAD_DIM)
            o, l = _attn_head(q_ref[:, sl], k, v, km, vm, meta_bias, None)
            o_ref[:, sl] = o / l


def _attention(q, k, v, km, vm, b, s, tq):
    t, d_attn = q.shape
    wq = Q_PER_KV * HEAD_DIM
    nq = s // tq
    mp = km.shape[0]
    return pl.pallas_call(
        _attn_kernel,
        out_shape=jax.ShapeDtypeStruct((t, d_attn), F32),
        grid=(b, N_KV, nq),
        in_specs=[pl.BlockSpec((tq, wq), lambda bi, kh, qi: (bi * nq + qi, kh)),
                  pl.BlockSpec((s, HEAD_DIM), lambda bi, kh, qi: (bi, kh)),
                  pl.BlockSpec((s, HEAD_DIM), lambda bi, kh, qi: (bi, kh)),
                  pl.BlockSpec((mp, HEAD_DIM), lambda bi, kh, qi: (0, kh)),
                  pl.BlockSpec((mp, HEAD_DIM), lambda bi, kh, qi: (0, kh))],
        out_specs=pl.BlockSpec((tq, wq), lambda bi, kh, qi: (bi * nq + qi, kh)),
        scratch_shapes=[pltpu.VMEM((1, HEAD_DIM), F32)],
        compiler_params=_cparams(("parallel", "parallel", "arbitrary")),
        name="attention",
    )(q, k, v, km, vm)


def _lru_kernel(xr_ref, yr_ref, xm_ref, cw_ref, cb_ref, wg_ref, bg_ref, lam_ref, o_ref,
                xpad, af, bf, ab, bb, *, s, rc):
    l = s + N_META
    bw = LRU_BW
    xpad[0:8, :] = jnp.zeros((8, bw), F32)
    xpad[8:8 + N_META, :] = xm_ref[...]
    xpad[8 + N_META:8 + l, :] = xr_ref[...]
    xpad[8 + l:16 + l, :] = jnp.zeros((8, bw), F32)

    lam = lam_ref[...]
    neg = -lam
    sp = jnp.maximum(neg, 0.0) + jnp.log1p(jnp.exp(-jnp.abs(neg)))
    cw = cw_ref[...]
    cb = cb_ref[...]
    wg = wg_ref[...]
    bg = bg_ref[...]

    def gates(t0, n):
        w = xpad[pl.ds(t0, n + 16), :]
        xc = (cw[0:1] * w[6:6 + n] + cw[1:2] * w[7:7 + n]
              + cw[2:3] * w[8:8 + n] + cw[3:4] * w[9:9 + n]) + cb
        gt = jnp.dot(xc.astype(BF16), wg, preferred_element_type=F32) + bg
        hx = 0.5 * xc
        for d, (a_ref, b_ref) in enumerate(((af, bf), (ab, bb))):
            tr = jnp.tanh(gt[:, (2 * d) * bw:(2 * d + 1) * bw])
            ti = jnp.tanh(gt[:, (2 * d + 1) * bw:(2 * d + 2) * bw])
            hc = (-0.5 * LRU_C) * sp[d:d + 1]
            log_a = hc + hc * tr
            a = jnp.exp(log_a)
            om = 1.0 - a * a
            root = jnp.where(om > 0.0, om * lax.rsqrt(om), 0.0)
            a_ref[pl.ds(t0, n), :] = a
            b_ref[pl.ds(t0, n), :] = root * (hx + hx * ti)

    gates(0, N_META)

    def gate_body(c, carry):
        gates(pl.multiple_of(N_META + c * rc, 8), rc)
        return carry

    lax.fori_loop(0, s // rc, gate_body, 0)

    nch = 8
    cl = l // nch

    def rows(ref, i):
        return ref.at[pl.ds(i, nch, stride=cl), :]

    def pass1(i, carry):
        hf, pf, hb, pb = carry
        a = rows(af, i)[...]
        hf = a * hf + rows(bf, i)[...]
        pf = a * pf
        rows(bf, i)[...] = hf
        rows(af, i)[...] = pf
        j = cl - 1 - i
        a = rows(ab, j)[...]
        hb = a * hb + rows(bb, j)[...]
        pb = a * pb
        rows(bb, j)[...] = hb
        rows(ab, j)[...] = pb
        return hf, pf, hb, pb

    zero = jnp.zeros((nch, bw), F32)
    one = jnp.ones((nch, bw), F32)
    hf, pf, hb, pb = lax.fori_loop(0, cl, pass1, (zero, one, zero, one), unroll=8)

    cf = [jnp.zeros((1, bw), F32)]
    for c in range(1, nch):
        cf.append(hf[c - 1:c] + pf[c - 1:c] * cf[-1])
    cin_f = jnp.concatenate(cf, axis=0)
    cbk = [jnp.zeros((1, bw), F32)]
    for c in range(nch - 2, -1, -1):
        cbk.append(hb[c + 1:c + 2] + pb[c + 1:c + 2] * cbk[-1])
    cin_b = jnp.concatenate(cbk[::-1], axis=0)

    def pass2(i, carry):
        rows(bf, i)[...] = rows(bf, i)[...] + rows(af, i)[...] * cin_f
        rows(bb, i)[...] = rows(bb, i)[...] + rows(ab, i)[...] * cin_b
        return carry

    lax.fori_loop(0, cl, pass2, 0, unroll=8)

    for c in range(s // rc):
        r0 = c * rc
        hsum = bf[N_META + r0:N_META + r0 + rc, :] + bb[N_META + r0:N_META + r0 + rc, :]
        o_ref[r0:r0 + rc, :] = hsum * _gelu(yr_ref[r0:r0 + rc, :])


def _lru(z, zm, cw, cb, wg, bg, lam, b, s, off_x, off_y, d_rnn):
    nblk = d_rnn // LRU_BW
    rc = min(256, s)
    l = s + N_META
    bx = off_x // LRU_BW
    by = off_y // LRU_BW
    return pl.pallas_call(
        functools.partial(_lru_kernel, s=s, rc=rc),
        out_shape=jax.ShapeDtypeStruct((b * s, d_rnn), F32),
        grid=(b, nblk),
        in_specs=[pl.BlockSpec((s, LRU_BW), lambda bi, n: (bi, bx + n)),
                  pl.BlockSpec((s, LRU_BW), lambda bi, n: (bi, by + n)),
                  pl.BlockSpec((N_META, LRU_BW), lambda bi, n: (0, bx + n)),
                  pl.BlockSpec((CONV_W, LRU_BW), lambda bi, n: (0, n)),
                  pl.BlockSpec((1, LRU_BW), lambda bi, n: (0, n)),
                  pl.BlockSpec((None, LRU_BW, 4 * LRU_BW), lambda bi, n: (n, 0, 0)),
                  pl.BlockSpec((None, 1, 4 * LRU_BW), lambda bi, n: (n, 0, 0)),
                  pl.BlockSpec((2, LRU_BW), lambda bi, n: (0, n))],
        out_specs=pl.BlockSpec((s, LRU_BW), lambda bi, n: (bi, n)),
        scratch_shapes=[pltpu.VMEM((l + 16, LRU_BW), F32)] + [pltpu.VMEM((l, LRU_BW), F32)] * 4,
        compiler_params=_cparams(("parallel", "parallel")),
        name="rg_lru",
    )(z, z, zm, cw, cb, wg, bg, lam)


def _out_proj_kernel(a_ref, r_ref, ga_ref, gr_ref, w_ref, x_ref, o_ref, m_ref):
    da = a_ref.shape[1]

    @pl.when(pl.program_id(1) == 0)
    def _():
        m_ref[:, 0:da] = _rms(a_ref[...], ga_ref[...]).astype(BF16)
        m_ref[:, da:] = _rms(r_ref[...], gr_ref[...]).astype(BF16)

    o_ref[...] = x_ref[...] + jnp.dot(m_ref[...], w_ref[...], preferred_element_type=F32)


def _out_proj(attn, lru, ga, gr, w, x2d, tm, tn):
    t, da = attn.shape
    dr = lru.shape[1]
    d = w.shape[1]
    return pl.pallas_call(
        _out_proj_kernel,
        out_shape=jax.ShapeDtypeStruct((t, d), F32),
        grid=(t // tm, d // tn),
        in_specs=[pl.BlockSpec((tm, da), lambda i, j: (i, 0)),
                  pl.BlockSpec((tm, dr), lambda i, j: (i, 0)),
                  pl.BlockSpec((1, da), lambda i, j: (0, 0)),
                  pl.BlockSpec((1, dr), lambda i, j: (0, 0)),
                  pl.BlockSpec((da + dr, tn), lambda i, j: (0, j)),
                  pl.BlockSpec((tm, tn), lambda i, j: (i, j))],
        out_specs=pl.BlockSpec((tm, tn), lambda i, j: (i, j)),
        scratch_shapes=[pltpu.VMEM((tm, da + dr), BF16)],
        compiler_params=_cparams(("parallel", "arbitrary")),
        name="out_proj",
    )(attn, lru, ga, gr, w, x2d)


def _topk_rows(sc, k):
    n, tm = sc.shape
    row = lax.broadcasted_iota(I32, (n, tm), 0)
    cur = sc
    rank = jnp.full((n, tm), k, I32)
    vals = []
    for r in range(k):
        mx = jnp.max(cur, axis=0, keepdims=True)
        idx = jnp.min(jnp.where(cur == mx, row, n), axis=0, keepdims=True)
        sel = row == idx
        rank = jnp.where(sel, r, rank)
        cur = jnp.where(sel, -jnp.inf, cur)
        vals.append(mx)
    return jnp.concatenate(vals, axis=0), rank


def _topk_rows_distinct(sc, k):
    cur = sc
    rank = jnp.full(sc.shape, k, I32)
    vals = []
    for r in range(k):
        mx = jnp.max(cur, axis=0, keepdims=True)
        sel = cur == mx
        rank = jnp.where(sel, r, rank)
        cur = jnp.where(sel, -jnp.inf, cur)
        vals.append(mx)
    return jnp.concatenate(vals, axis=0), rank


def _count_rows(x):
    return jnp.sum(x.astype(I32), axis=0, keepdims=True)


def _route_kernel(h_ref, g_ref, wq_ref, sk_ref, xn_ref, e1_ref, c1_ref, e2_ref, r2_ref,
                  q_scr, r1_s, r2_s, cnt_s, z_s):
    k = PEER_TOPK
    xn = _rms(h_ref[...], g_ref[...]).astype(BF16)
    xn_ref[...] = xn
    q = jnp.dot(xn, wq_ref[...], preferred_element_type=F32).astype(BF16)
    tm = q.shape[0]
    for c in range(2 * PEER_HEADS):
        q_scr[c] = q[:, c * N_KEYS:(c + 1) * N_KEYS]

    hk = k // 2
    pos_col = jnp.concatenate(
        [lax.broadcasted_iota(I32, (k, 1), 0)]
        + [a * k + lax.broadcasted_iota(I32, (hk, 1), 0) for a in range(1, k)], axis=0)
    nc = pos_col.shape[0]
    arow = lax.broadcasted_iota(I32, (k, tm), 0)

    def candidates(v1, v2):
        return jnp.concatenate(
            [v1[0:1] + v2] + [v1[a:a + 1] + v2[0:hk] for a in range(1, k)], axis=0)

    def head_body(h, carry):
        s1 = _dot_nt(sk_ref[2 * h], q_scr[2 * h])
        s2 = _dot_nt(sk_ref[2 * h + 1], q_scr[2 * h + 1])

        v1, r1 = _topk_rows_distinct(s1, k)
        v2, r2 = _topk_rows_distinct(s2, k)
        cand = candidates(v1, v2)
        cur = cand
        top = v1[0:1] + v2[0:1]
        zsum = jnp.zeros((1, tm), F32)
        mx = top
        for j in range(k):
            mx = jnp.max(cur, axis=0, keepdims=True)
            cur = jnp.where(cur == mx, -jnp.inf, cur)
            zsum = zsum + jnp.exp(mx - top)
        selc = cand >= mx
        cnt = jnp.concatenate(
            [_count_rows(selc[0:k])]
            + [_count_rows(selc[k + hk * (a - 1):k + hk * a]) for a in range(1, k)], axis=0)
        bad = jnp.logical_or(
            jnp.logical_or(_count_rows(r1 < k) != k, _count_rows(r2 < k) != k),
            _count_rows(cnt) != k)
        r1_s[...] = r1
        r2_s[...] = r2
        cnt_s[...] = cnt
        z_s[...] = zsum

        @pl.when(jnp.max(bad.astype(I32)) > 0)
        def _():
            v1x, r1x = _topk_rows(s1, k)
            v2x, r2x = _topk_rows(s2, k)
            pos = jnp.broadcast_to(pos_col, (nc, tm))
            curx = candidates(v1x, v2x)
            mask_a = jnp.zeros((k, tm), I32)
            zx = jnp.zeros((1, tm), F32)
            for j in range(k):
                mxx = jnp.max(curx, axis=0, keepdims=True)
                pj = jnp.min(jnp.where(curx == mxx, pos, k * k), axis=0, keepdims=True)
                curx = jnp.where(pos == pj, -jnp.inf, curx)
                zx = zx + jnp.exp(mxx - top)
                aj = jnp.right_shift(pj, 4)
                bj = jnp.bitwise_and(pj, k - 1)
                mask_a = jnp.where(arow == aj, jnp.bitwise_or(mask_a, jnp.left_shift(1, bj)), mask_a)
            r1_s[...] = r1x
            r2_s[...] = r2x
            cnt_s[...] = lax.population_count(mask_a)
            z_s[...] = zx

        r1f = r1_s[...]
        cntf = cnt_s[...]
        c1 = jnp.zeros(r1f.shape, I32)
        for a in range(k):
            c1 = jnp.where(r1f == a, cntf[a:a + 1], c1)
        e1_ref[h] = jnp.exp(s1 - v1[0:1] - jnp.log(z_s[...]))
        c1_ref[h] = c1.astype(F32)
        e2_ref[h] = jnp.exp(s2 - v2[0:1]).astype(BF16)
        r2_ref[h] = r2_s[...].astype(F32).astype(BF16)
        return carry

    lax.fori_loop(0, PEER_HEADS, head_body, 0)


def _route(h1, g, wq, sk, tm):
    t, d = h1.shape
    nq = wq.shape[1]
    rf = jax.ShapeDtypeStruct((PEER_HEADS, N_KEYS, t), F32)
    rb = jax.ShapeDtypeStruct((PEER_HEADS, N_KEYS, t), BF16)
    rspec = pl.BlockSpec((PEER_HEADS, N_KEYS, tm), lambda i: (0, 0, i))
    return pl.pallas_call(
        _route_kernel,
        out_shape=(jax.ShapeDtypeStruct((t, d), BF16), rf, rf, rb, rb),
        grid=(t // tm,),
        in_specs=[pl.BlockSpec((tm, d), lambda i: (i, 0)),
                  pl.BlockSpec((1, d), lambda i: (0, 0)),
                  pl.BlockSpec((d, nq), lambda i: (0, 0), pipeline_mode=pl.Buffered(1)),
                  pl.BlockSpec((2 * PEER_HEADS, N_KEYS, N_KEYS), lambda i: (0, 0, 0))],
        out_specs=(pl.BlockSpec((tm, d), lambda i: (i, 0)), rspec, rspec, rspec, rspec),
        scratch_shapes=[pltpu.VMEM((2 * PEER_HEADS, tm, N_KEYS), BF16),
                        pltpu.VMEM((N_KEYS, tm), I32), pltpu.VMEM((N_KEYS, tm), I32),
                        pltpu.VMEM((PEER_TOPK, tm), I32), pltpu.VMEM((1, tm), F32)],
        compiler_params=_cparams(("parallel",)),
        name="peer_route",
    )(h1, g, wq, sk)


def _gelu_rcp(x):
    k0 = -2.0 * math.sqrt(2.0 / math.pi)
    return x / (1.0 + jnp.exp(x * (k0 + (k0 * 0.044715) * (x * x))))


def _rows_bf16(x):
    t16 = jnp.broadcast_to(x, (16, x.shape[1])).astype(BF16)
    return jnp.concatenate([t16] * (N_KEYS // 16), axis=0)


def _peer_kernel(xn_ref, h_ref, u_ref, v_ref, e1_ref, c1_ref, e2_ref, r2_ref, o_ref, *, nb):
    e = pl.program_id(1)

    slab = h_ref.shape[1]
    nslab = o_ref.shape[1] // slab

    @pl.when(e == 0)
    def _():
        o_ref[:, 0:slab] = h_ref[...]
        o_ref[:, slab:] = jnp.zeros((o_ref.shape[0], o_ref.shape[1] - slab), F32)

    for cs in range(1, nslab):
        @pl.when(e == cs)
        def _(cs=cs):
            o_ref[:, cs * slab:(cs + 1) * slab] += h_ref[...]

    act = jnp.dot(u_ref[...], xn_ref[...], preferred_element_type=F32)
    ga = _gelu_rcp(act)
    tm = ga.shape[1]
    parts = []
    for kb in range(nb):
        n1 = e * nb + kb
        w = jnp.zeros((N_KEYS, tm), BF16)
        for h in range(PEER_HEADS):
            c1 = _rows_bf16(c1_ref[h, pl.ds(n1, 1), :])
            e1 = _rows_bf16(e1_ref[h, pl.ds(n1, 1), :])
            w = w + jnp.where(r2_ref[h] < c1, e1 * e2_ref[h], jnp.zeros_like(e1))
        parts.append((w.astype(F32) * ga[kb * N_KEYS:(kb + 1) * N_KEYS, :]).T.astype(BF16))
    wa = jnp.concatenate(parts, axis=1)
    o_ref[...] += jnp.dot(wa, v_ref[...], preferred_element_type=F32)


def _peer(xnt, h1, u, v, e1, c1, e2, r2, tm, ec):
    d, t = xnt.shape
    nchunk = u.shape[0] // ec
    nb = ec // N_KEYS
    nslab = 4
    assert nchunk >= nslab and d % (nslab * 128) == 0
    rspec = pl.BlockSpec((PEER_HEADS, N_KEYS, tm), lambda i, e: (0, 0, i))
    return pl.pallas_call(
        functools.partial(_peer_kernel, nb=nb),
        out_shape=jax.ShapeDtypeStruct((t, d), F32),
        grid=(t // tm, nchunk),
        in_specs=[pl.BlockSpec((d, tm), lambda i, e: (0, i)),
                  pl.BlockSpec((tm, d // nslab), lambda i, e: (i, jnp.minimum(e, nslab - 1))),
                  pl.BlockSpec((ec, d), lambda i, e: (e, 0)),
                  pl.BlockSpec((ec, d), lambda i, e: (e, 0)),
                  rspec, rspec, rspec, rspec],
        out_specs=pl.BlockSpec((tm, d), lambda i, e: (i, 0)),
        compiler_params=_cparams(("parallel", "arbitrary")),
        name="peer_experts",
    )(xnt, h1, u, v, e1, c1, e2, r2)


def _rope_tables(s):
    rows = s // GRID_W
    row = jnp.repeat(jnp.arange(rows, dtype=F32), GRID_W)
    col = (jnp.arange(rows * GRID_W) % GRID_W).astype(F32)
    inv = ROPE_THETA ** (-jnp.arange(ROPE_PAIRS, dtype=F32) / ROPE_PAIRS)
    ar = row[:, None] * inv[None, :]
    ac = col[:, None] * inv[None, :]
    ang = jnp.concatenate([ar, ar, ac, ac], axis=-1)
    lane = jnp.arange(HEAD_DIM)
    sign = jnp.where((lane % (2 * ROPE_PAIRS)) < ROPE_PAIRS, -1.0, 1.0).astype(F32)
    return jnp.cos(ang), jnp.sin(ang) * sign[None, :]


def _tile(n, pref):
    t = min(pref, n)
    while n % t:
        t //= 2
    return t


def kernel(x, meta_tokens, norm1_g, w_in, q_norm_g, k_norm_g, conv_w, conv_b, w_rg, b_rg, w_ig,
           b_ig, lru_lambda, attn_out_g, lru_out_g, w_out, norm2_g, peer_wq, peer_subkeys,
           peer_u, peer_v):
    b, s, d = x.shape
    t = b * s
    depth = w_in.shape[0]
    assert depth == 1, "meta rows of the stream are only materialised as attention / recurrence context"
    d_attn = attn_out_g.shape[-1]
    d_rnn = lru_out_g.shape[-1]
    d_kv = N_KV * HEAD_DIM
    off_x = d_attn + 2 * d_kv
    off_y = off_x + d_rnn
    nblk = d_rnn // LRU_BW

    x2d = x.reshape(t, d)
    meta = meta_tokens.astype(x.dtype)
    g1 = norm1_g[0].reshape(1, d)
    w_in_b = w_in[0].astype(BF16)

    tm = _tile(t, 512)
    z = _in_proj(x2d, g1, w_in_b, tm, 1024)
    zm = _in_proj(meta, g1, w_in_b, N_META, 1024)

    cos, sin_s = _rope_tables(s)
    qg = q_norm_g[0].reshape(1, HEAD_DIM)
    kg = k_norm_g[0].reshape(1, HEAD_DIM)
    q, k, v = _qk_prep(z, cos, sin_s, qg, kg, _tile(s, 256), d_attn)
    _, km, vm = _qk_prep(zm, jnp.ones((N_META, HEAD_DIM), F32), jnp.zeros((N_META, HEAD_DIM), F32),
                         qg, kg, N_META, d_attn)
    pad = ((0, N_KEYS - N_META), (0, 0))
    attn = _attention(q, k, v, jnp.pad(km, pad), jnp.pad(vm, pad), b, s, _tile(s, 256))

    wg = jnp.concatenate([w_rg[0, 0], w_ig[0, 0], w_rg[0, 1], w_ig[0, 1]], axis=-1)
    wg = (0.5 * wg).astype(BF16)
    bg = 0.5 * jnp.stack([b_rg[0, 0], b_ig[0, 0], b_rg[0, 1], b_ig[0, 1]], axis=0)
    bg = bg.reshape(4, nblk, LRU_BW).transpose(1, 0, 2).reshape(nblk, 1, 4 * LRU_BW)
    lru = _lru(z, zm, conv_w[0], conv_b[0].reshape(1, d_rnn), wg, bg, lru_lambda[0],
               b, s, off_x, off_y, d_rnn)

    h1 = _out_proj(attn, lru, attn_out_g[0].reshape(1, d_attn), lru_out_g[0].reshape(1, d_rnn),
                   w_out[0].astype(BF16), x2d, tm, 1024)

    sk = peer_subkeys[0].reshape(2 * PEER_HEADS, N_KEYS, -1).astype(BF16)
    xn2, e1, c1, e2, r2 = _route(h1, norm2_g[0].reshape(1, d), peer_wq[0].astype(BF16), sk,
                                 _tile(t, 256))
    out = _peer(xn2.T, h1, peer_u[0].astype(BF16), peer_v[0].astype(BF16), e1, c1, e2, r2,
                _tile(t, 512), 512)
    return out.reshape(b, s, d)
```

```python
import functools
import math

import jax
import jax.numpy as jnp
from jax import lax
from jax.experimental import pallas as pl
from jax.experimental.pallas import tpu as pltpu

F32 = jnp.float32
BF16 = jnp.bfloat16
I32 = jnp.int32

EPS = 1e-6
N_META = 16
GRID_W = 64
HEAD_DIM = 128
N_KV = 4
Q_PER_KV = 4
ROPE_THETA = 10000.0
ROPE_PAIRS = HEAD_DIM // 4
LRU_BW = 128
LRU_C = 8.0
CONV_W = 4
PEER_HEADS = 8
N_KEYS = 128
PEER_TOPK = 16
ATTN_SCALE = 1.0 / math.sqrt(HEAD_DIM)
NEG_BIG = -1e30

VMEM_LIMIT = 60 * 1024 * 1024


def _cparams(sem, flags=None):
    return pltpu.CompilerParams(dimension_semantics=sem, vmem_limit_bytes=VMEM_LIMIT, flags=flags)


def _rms(x, g):
    ms = jnp.mean(x * x, axis=-1, keepdims=True)
    return x * lax.rsqrt(ms + EPS) * g


def _gelu(x):
    c = math.sqrt(2.0 / math.pi)
    return 0.5 * x * (1.0 + jnp.tanh(c * (x + 0.044715 * (x * x * x))))


def _dot_nt(a, b):
    return lax.dot_general(a, b, (((1,), (1,)), ((), ())), preferred_element_type=F32)


def _in_proj_kernel(x_ref, g_ref, w_ref, o_ref, xn_ref):
    @pl.when(pl.program_id(1) == 0)
    def _():
        xn_ref[...] = _rms(x_ref[...], g_ref[...]).astype(BF16)

    o_ref[...] = jnp.dot(xn_ref[...], w_ref[...], preferred_element_type=F32)


def _in_proj(x2d, g, w, tm, tn):
    t, d = x2d.shape
    n = w.shape[1]
    return pl.pallas_call(
        _in_proj_kernel,
        out_shape=jax.ShapeDtypeStruct((t, n), F32),
        grid=(t // tm, n // tn),
        in_specs=[pl.BlockSpec((tm, d), lambda i, j: (i, 0)),
                  pl.BlockSpec((1, d), lambda i, j: (0, 0)),
                  pl.BlockSpec((d, tn), lambda i, j: (0, j))],
        out_specs=pl.BlockSpec((tm, tn), lambda i, j: (i, j)),
        scratch_shapes=[pltpu.VMEM((tm, d), BF16)],
        compiler_params=_cparams(("parallel", "arbitrary")),
        name="in_proj",
    )(x2d, g, w)


def _qk_prep_kernel(zqk_ref, zv_ref, cos_ref, sin_ref, qg_ref, kg_ref,
                    q_ref, k_ref, v_ref, *, n_q):
    cos = cos_ref[...]
    sin = sin_ref[...]
    lane = lax.broadcasted_iota(I32, cos.shape, 1)
    lo = (lane % (2 * ROPE_PAIRS)) < ROPE_PAIRS

    def norm_rope(t, g):
        tn = _rms(t, g)
        rot = jnp.where(lo, pltpu.roll(tn, HEAD_DIM - ROPE_PAIRS, 1), pltpu.roll(tn, ROPE_PAIRS, 1))
        return (tn * cos + rot * sin).astype(BF16)

    for h in range(n_q):
        sl = slice(h * HEAD_DIM, (h + 1) * HEAD_DIM)
        q_ref[:, sl] = norm_rope(zqk_ref[:, sl], qg_ref[...])
    for h in range(N_KV):
        src = slice((n_q + h) * HEAD_DIM, (n_q + h + 1) * HEAD_DIM)
        k_ref[:, h * HEAD_DIM:(h + 1) * HEAD_DIM] = norm_rope(zqk_ref[:, src], kg_ref[...])
    v_ref[...] = zv_ref[...].astype(BF16)


def _qk_prep(z, cos, sin_signed, qg, kg, tm, d_attn):
    t = z.shape[0]
    n_q = d_attn // HEAD_DIM
    d_kv = N_KV * HEAD_DIM
    w_qk = d_attn + d_kv
    n_tab = cos.shape[0] // tm
    return pl.pallas_call(
        functools.partial(_qk_prep_kernel, n_q=n_q),
        out_shape=(jax.ShapeDtypeStruct((t, d_attn), BF16),
                   jax.ShapeDtypeStruct((t, d_kv), BF16),
                   jax.ShapeDtypeStruct((t, d_kv), BF16)),
        grid=(t // tm,),
        in_specs=[pl.BlockSpec((tm, w_qk), lambda i: (i, 0)),
                  pl.BlockSpec((tm, d_kv), lambda i: (i, w_qk // d_kv)),
                  pl.BlockSpec((tm, HEAD_DIM), lambda i: (i % n_tab, 0)),
                  pl.BlockSpec((tm, HEAD_DIM), lambda i: (i % n_tab, 0)),
                  pl.BlockSpec((1, HEAD_DIM), lambda i: (0, 0)),
                  pl.BlockSpec((1, HEAD_DIM), lambda i: (0, 0))],
        out_specs=(pl.BlockSpec((tm, d_attn), lambda i: (i, 0)),
                   pl.BlockSpec((tm, d_kv), lambda i: (i, 0)),
                   pl.BlockSpec((tm, d_kv), lambda i: (i, 0))),
        compiler_params=_cparams(("parallel",)),
        name="qk_prep",
    )(z, z, cos, sin_signed, qg, kg)


def _attn_head(q, k, v, km, vm, meta_bias, mc):
    c = ATTN_SCALE * math.log2(math.e)
    ns = k.shape[0]
    kc = min(ns, 512)
    lw = km.shape[0]
    sm = _dot_nt(q, km) + meta_bias
    if mc is None:
        s = _dot_nt(q, k)
        chunks = [s[:, j * kc:(j + 1) * kc] for j in range(ns // kc)]
        mx = sm
        for j in range(ns // lw):
            mx = jnp.maximum(mx, s[:, j * lw:(j + 1) * lw])
        mc = jnp.max(mx, axis=-1, keepdims=True) * c
    else:
        chunks = [_dot_nt(q, k[j * kc:(j + 1) * kc, :]) for j in range(ns // kc)]
    pm = jnp.exp2(sm * c - mc)
    lsum = pm
    o = jnp.dot(pm.astype(BF16), vm, preferred_element_type=F32)
    for j in range(ns // kc):
        parts = []
        for i in range(kc // lw):
            p = jnp.exp2(chunks[j][:, i * lw:(i + 1) * lw] * c - mc)
            lsum = lsum + p
            parts.append(p.astype(BF16))
        o = o + jnp.dot(jnp.concatenate(parts, axis=1), v[j * kc:(j + 1) * kc, :],
                        preferred_element_type=F32)
    return o, jnp.sum(lsum, axis=-1, keepdims=True)


ATTN_MIN_ROW_SUM = 2.0 ** -60


def _attn_kernel(q_ref, k_ref, v_ref, km_ref, vm_ref, o_ref, kmax_scr):
    k = k_ref[...]
    v = v_ref[...]
    km = km_ref[...]
    vm = vm_ref[...]
    col = lax.broadcasted_iota(I32, (1, km.shape[0]), 1)
    meta_bias = jnp.where(col < N_META, 0.0, NEG_BIG).astype(F32)
    c = ATTN_SCALE * math.log2(math.e)

    @pl.when(pl.program_id(2) == 0)
    def _():
        kf = k.astype(F32)
        kmf = km.astype(F32)
        n2 = jnp.maximum(jnp.max(jnp.sum(kf * kf, axis=-1, keepdims=True), axis=0, keepdims=True),
                         jnp.max(jnp.sum(kmf * kmf, axis=-1, keepdims=True), axis=0, keepdims=True))
        kmax_scr[...] = jnp.broadcast_to(jnp.sqrt(n2), kmax_scr.shape)

    kmax_c = kmax_scr[:, 0:1] * c
    lmin = None
    for g in range(Q_PER_KV):
        sl = slice(g * HEAD_DIM, (g + 1) * HEAD_DIM)
        q = q_ref[:, sl]
        qf = q.astype(F32)
        mc = jnp.sqrt(jnp.sum(qf * qf, axis=-1, keepdims=True)) * kmax_c
        o, l = _attn_head(q, k, v, km, vm, meta_bias, mc)
        o_ref[:, sl] = o / l
        lg = jnp.min(l)
        lmin = lg if lmin is None else jnp.minimum(lmin, lg)

    @pl.when(jnp.logical_not(lmin >= ATTN_MIN_ROW_SUM))
    def _():
        for g in range(Q_PER_KV):
            sl = slice(g * HEAD_DIM, (g + 1) * HEAD_DIM)
            o, l = _attn_head(q_ref[:, sl], k, v, km, vm, meta_bias, None)
            o_ref[:, sl] = o / l


def _attention(q, k, v, km, vm, b, s, tq):
    t, d_attn = q.shape
    wq = Q_PER_KV * HEAD_DIM
    nq = s // tq
    mp = km.shape[0]
    return pl.pallas_call(
        _attn_kernel,
        out_shape=jax.ShapeDtypeStruct((t, d_attn), F32),
        grid=(b, N_KV, nq),
        in_specs=[pl.BlockSpec((tq, wq), lambda bi, kh, qi: (bi * nq + qi, kh)),
                  pl.BlockSpec((s, HEAD_DIM), lambda bi, kh, qi: (bi, kh)),
                  pl.BlockSpec((s, HEAD_DIM), lambda bi, kh, qi: (bi, kh)),
                  pl.BlockSpec((mp, HEAD_DIM), lambda bi, kh, qi: (0, kh)),
                  pl.BlockSpec((mp, HEAD_DIM), lambda bi, kh, qi: (0, kh))],
        out_specs=pl.BlockSpec((tq, wq), lambda bi, kh, qi: (bi * nq + qi, kh)),
        scratch_shapes=[pltpu.VMEM((1, HEAD_DIM), F32)],
        compiler_params=_cparams(("parallel", "parallel", "arbitrary")),
        name="attention",
    )(q, k, v, km, vm)


def _lru_kernel(xr_ref, yr_ref, xm_ref, cw_ref, cb_ref, wg_ref, bg_ref, lam_ref, o_ref,
                xpad, af, bf, ab, bb, *, s, rc):
    l = s + N_META
    bw = LRU_BW
    xpad[0:8, :] = jnp.zeros((8, bw), F32)
    xpad[8:8 + N_META, :] = xm_ref[...]
    xpad[8 + N_META:8 + l, :] = xr_ref[...]
    xpad[8 + l:16 + l, :] = jnp.zeros((8, bw), F32)

    lam = lam_ref[...]
    neg = -lam
    sp = jnp.maximum(neg, 0.0) + jnp.log1p(jnp.exp(-jnp.abs(neg)))
    cw = cw_ref[...]
    cb = cb_ref[...]
    wg = wg_ref[...]
    bg = bg_ref[...]

    def gates(t0, n):
        w = xpad[pl.ds(t0, n + 16), :]
        xc = (cw[0:1] * w[6:6 + n] + cw[1:2] * w[7:7 + n]
              + cw[2:3] * w[8:8 + n] + cw[3:4] * w[9:9 + n]) + cb
        gt = jnp.dot(xc.astype(BF16), wg, preferred_element_type=F32) + bg
        hx = 0.5 * xc
        for d, (a_ref, b_ref) in enumerate(((af, bf), (ab, bb))):
            tr = jnp.tanh(gt[:, (2 * d) * bw:(2 * d + 1) * bw])
            ti = jnp.tanh(gt[:, (2 * d + 1) * bw:(2 * d + 2) * bw])
            hc = (-0.5 * LRU_C) * sp[d:d + 1]
            log_a = hc + hc * tr
            a = jnp.exp(log_a)
            om = 1.0 - a * a
            root = jnp.where(om > 0.0, om * lax.rsqrt(om), 0.0)
            a_ref[pl.ds(t0, n), :] = a
            b_ref[pl.ds(t0, n), :] = root * (hx + hx * ti)

    gates(0, N_META)

    def gate_body(c, carry):
        gates(pl.multiple_of(N_META + c * rc, 8), rc)
        return carry

    lax.fori_loop(0, s // rc, gate_body, 0)

    nch = 8
    cl = l // nch

    def rows(ref, i):
        return ref.at[pl.ds(i, nch, stride=cl), :]

    def pass1(i, carry):
        hf, pf, hb, pb = carry
        a = rows(af, i)[...]
        hf = a * hf + rows(bf, i)[...]
        pf = a * pf
        rows(bf, i)[...] = hf
        rows(af, i)[...] = pf
        j = cl - 1 - i
        a = rows(ab, j)[...]
        hb = a * hb + rows(bb, j)[...]
        pb = a * pb
        rows(bb, j)[...] = hb
        rows(ab, j)[...] = pb
        return hf, pf, hb, pb

    zero = jnp.zeros((nch, bw), F32)
    one = jnp.ones((nch, bw), F32)
    hf, pf, hb, pb = lax.fori_loop(0, cl, pass1, (zero, one, zero, one), unroll=8)

    cf = [jnp.zeros((1, bw), F32)]
    for c in range(1, nch):
        cf.append(hf[c - 1:c] + pf[c - 1:c] * cf[-1])
    cin_f = jnp.concatenate(cf, axis=0)
    cbk = [jnp.zeros((1, bw), F32)]
    for c in range(nch - 2, -1, -1):
        cbk.append(hb[c + 1:c + 2] + pb[c + 1:c + 2] * cbk[-1])
    cin_b = jnp.concatenate(cbk[::-1], axis=0)

    def pass2(i, carry):
        rows(bf, i)[...] = rows(bf, i)[...] + rows(af, i)[...] * cin_f
        rows(bb, i)[...] = rows(bb, i)[...] + rows(ab, i)[...] * cin_b
        return carry

    lax.fori_loop(0, cl, pass2, 0, unroll=8)

    for c in range(s // rc):
        r0 = c * rc
        hsum = bf[N_META + r0:N_META + r0 + rc, :] + bb[N_META + r0:N_META + r0 + rc, :]
        o_ref[r0:r0 + rc, :] = hsum * _gelu(yr_ref[r0:r0 + rc, :])


def _lru(z, zm, cw, cb, wg, bg, lam, b, s, off_x, off_y, d_rnn):
    nblk = d_rnn // LRU_BW
    rc = min(256, s)
    l = s + N_META
    bx = off_x // LRU_BW
    by = off_y // LRU_BW
    return pl.pallas_call(
        functools.partial(_lru_kernel, s=s, rc=rc),
        out_shape=jax.ShapeDtypeStruct((b * s, d_rnn), F32),
        grid=(b, nblk),
        in_specs=[pl.BlockSpec((s, LRU_BW), lambda bi, n: (bi, bx + n)),
                  pl.BlockSpec((s, LRU_BW), lambda bi, n: (bi, by + n)),
                  pl.BlockSpec((N_META, LRU_BW), lambda bi, n: (0, bx + n)),
                  pl.BlockSpec((CONV_W, LRU_BW), lambda bi, n: (0, n)),
                  pl.BlockSpec((1, LRU_BW), lambda bi, n: (0, n)),
                  pl.BlockSpec((None, LRU_BW, 4 * LRU_BW), lambda bi, n: (n, 0, 0)),
                  pl.BlockSpec((None, 1, 4 * LRU_BW), lambda bi, n: (n, 0, 0)),
                  pl.BlockSpec((2, LRU_BW), lambda bi, n: (0, n))],
        out_specs=pl.BlockSpec((s, LRU_BW), lambda bi, n: (bi, n)),
        scratch_shapes=[pltpu.VMEM((l + 16, LRU_BW), F32)] + [pltpu.VMEM((l, LRU_BW), F32)] * 4,
        compiler_params=_cparams(("parallel", "parallel")),
        name="rg_lru",
    )(z, z, zm, cw, cb, wg, bg, lam)


def _out_proj_kernel(a_ref, r_ref, ga_ref, gr_ref, w_ref, x_ref, o_ref, m_ref):
    da = a_ref.shape[1]

    @pl.when(pl.program_id(1) == 0)
    def _():
        m_ref[:, 0:da] = _rms(a_ref[...], ga_ref[...]).astype(BF16)
        m_ref[:, da:] = _rms(r_ref[...], gr_ref[...]).astype(BF16)

    o_ref[...] = x_ref[...] + jnp.dot(m_ref[...], w_ref[...], preferred_element_type=F32)


def _out_proj(attn, lru, ga, gr, w, x2d, tm, tn):
    t, da = attn.shape
    dr = lru.shape[1]
    d = w.shape[1]
    return pl.pallas_call(
        _out_proj_kernel,
        out_shape=jax.ShapeDtypeStruct((t, d), F32),
        grid=(t // tm, d // tn),
        in_specs=[pl.BlockSpec((tm, da), lambda i, j: (i, 0)),
                  pl.BlockSpec((tm, dr), lambda i, j: (i, 0)),
                  pl.BlockSpec((1, da), lambda i, j: (0, 0)),
                  pl.BlockSpec((1, dr), lambda i, j: (0, 0)),
                  pl.BlockSpec((da + dr, tn), lambda i, j: (0, j)),
                  pl.BlockSpec((tm, tn), lambda i, j: (i, j))],
        out_specs=pl.BlockSpec((tm, tn), lambda i, j: (i, j)),
        scratch_shapes=[pltpu.VMEM((tm, da + dr), BF16)],
        compiler_params=_cparams(("parallel", "arbitrary")),
        name="out_proj",
    )(attn, lru, ga, gr, w, x2d)


def _topk_rows(sc, k):
    n, tm = sc.shape
    row = lax.broadcasted_iota(I32, (n, tm), 0)
    cur = sc
    rank = jnp.full((n, tm), k, I32)
    vals = []
    for r in range(k):
        mx = jnp.max(cur, axis=0, keepdims=True)
        idx = jnp.min(jnp.where(cur == mx, row, n), axis=0, keepdims=True)
        sel = row == idx
        rank = jnp.where(sel, r, rank)
        cur = jnp.where(sel, -jnp.inf, cur)
        vals.append(mx)
    return jnp.concatenate(vals, axis=0), rank


def _topk_rows_distinct(sc, k):
    cur = sc
    rank = jnp.full(sc.shape, k, I32)
    vals = []
    for r in range(k):
        mx = jnp.max(cur, axis=0, keepdims=True)
        sel = cur == mx
        rank = jnp.where(sel, r, rank)
        cur = jnp.where(sel, -jnp.inf, cur)
        vals.append(mx)
    return jnp.concatenate(vals, axis=0), rank


def _count_rows(x):
    return jnp.sum(x.astype(I32), axis=0, keepdims=True)


def _route_kernel(h_ref, g_ref, wq_ref, sk_ref, xn_ref, e1_ref, c1_ref, e2_ref, r2_ref,
                  q_scr, r1_s, r2_s, cnt_s, z_s):
    k = PEER_TOPK
    xn = _rms(h_ref[...], g_ref[...]).astype(BF16)
    xn_ref[...] = xn
    q = jnp.dot(xn, wq_ref[...], preferred_element_type=F32).astype(BF16)
    tm = q.shape[0]
    for c in range(2 * PEER_HEADS):
        q_scr[c] = q[:, c * N_KEYS:(c + 1) * N_KEYS]

    hk = k // 2
    pos_col = jnp.concatenate(
        [lax.broadcasted_iota(I32, (k, 1), 0)]
        + [a * k + lax.broadcasted_iota(I32, (hk, 1), 0) for a in range(1, k)], axis=0)
    nc = pos_col.shape[0]
    arow = lax.broadcasted_iota(I32, (k, tm), 0)

    def candidates(v1, v2):
        return jnp.concatenate(
            [v1[0:1] + v2] + [v1[a:a + 1] + v2[0:hk] for a in range(1, k)], axis=0)

    def head_body(h, carry):
        s1 = _dot_nt(sk_ref[2 * h], q_scr[2 * h])
        s2 = _dot_nt(sk_ref[2 * h + 1], q_scr[2 * h + 1])

        v1, r1 = _topk_rows_distinct(s1, k)
        v2, r2 = _topk_rows_distinct(s2, k)
        cand = candidates(v1, v2)
        cur = cand
        top = v1[0:1] + v2[0:1]
        zsum = jnp.zeros((1, tm), F32)
        mx = top
        for j in range(k):
            mx = jnp.max(cur, axis=0, keepdims=True)
            cur = jnp.where(cur == mx, -jnp.inf, cur)
            zsum = zsum + jnp.exp(mx - top)
        selc = cand >= mx
        cnt = jnp.concatenate(
            [_count_rows(selc[0:k])]
            + [_count_rows(selc[k + hk * (a - 1):k + hk * a]) for a in range(1, k)], axis=0)
        bad = jnp.logical_or(
            jnp.logical_or(_count_rows(r1 < k) != k, _count_rows(r2 < k) != k),
            _count_rows(cnt) != k)
        r1_s[...] = r1
        r2_s[...] = r2
        cnt_s[...] = cnt
        z_s[...] = zsum

        @pl.when(jnp.max(bad.astype(I32)) > 0)
        def _():
            v1x, r1x = _topk_rows(s1, k)
            v2x, r2x = _topk_rows(s2, k)
            pos = jnp.broadcast_to(pos_col, (nc, tm))
            curx = candidates(v1x, v2x)
            mask_a = jnp.zeros((k, tm), I32)
            zx = jnp.zeros((1, tm), F32)
            for j in range(k):
                mxx = jnp.max(curx, axis=0, keepdims=True)
                pj = jnp.min(jnp.where(curx == mxx, pos, k * k), axis=0, keepdims=True)
                curx = jnp.where(pos == pj, -jnp.inf, curx)
                zx = zx + jnp.exp(mxx - top)
                aj = jnp.right_shift(pj, 4)
                bj = jnp.bitwise_and(pj, k - 1)
                mask_a = jnp.where(arow == aj, jnp.bitwise_or(mask_a, jnp.left_shift(1, bj)), mask_a)
            r1_s[...] = r1x
            r2_s[...] = r2x
            cnt_s[...] = lax.population_count(mask_a)
            z_s[...] = zx

        r1f = r1_s[...]
        cntf = cnt_s[...]
        c1 = jnp.zeros(r1f.shape, I32)
        for a in range(k):
            c1 = jnp.where(r1f == a, cntf[a:a + 1], c1)
        e1_ref[h] = jnp.exp(s1 - v1[0:1] - jnp.log(z_s[...]))
        c1_ref[h] = c1.astype(F32)
        e2_ref[h] = jnp.exp(s2 - v2[0:1]).astype(BF16)
        r2_ref[h] = r2_s[...].astype(F32).astype(BF16)
        return carry

    lax.fori_loop(0, PEER_HEADS, head_body, 0)


def _route(h1, g, wq, sk, tm):
    t, d = h1.shape
    nq = wq.shape[1]
    rf = jax.ShapeDtypeStruct((PEER_HEADS, N_KEYS, t), F32)
    rb = jax.ShapeDtypeStruct((PEER_HEADS, N_KEYS, t), BF16)
    rspec = pl.BlockSpec((PEER_HEADS, N_KEYS, tm), lambda i: (0, 0, i))
    return pl.pallas_call(
        _route_kernel,
        out_shape=(jax.ShapeDtypeStruct((t, d), BF16), rf, rf, rb, rb),
        grid=(t // tm,),
        in_specs=[pl.BlockSpec((tm, d), lambda i: (i, 0)),
                  pl.BlockSpec((1, d), lambda i: (0, 0)),
                  pl.BlockSpec((d, nq), lambda i: (0, 0), pipeline_mode=pl.Buffered(1)),
                  pl.BlockSpec((2 * PEER_HEADS, N_KEYS, N_KEYS), lambda i: (0, 0, 0))],
        out_specs=(pl.BlockSpec((tm, d), lambda i: (i, 0)), rspec, rspec, rspec, rspec),
        scratch_shapes=[pltpu.VMEM((2 * PEER_HEADS, tm, N_KEYS), BF16),
                        pltpu.VMEM((N_KEYS, tm), I32), pltpu.VMEM((N_KEYS, tm), I32),
                        pltpu.VMEM((PEER_TOPK, tm), I32), pltpu.VMEM((1, tm), F32)],
        compiler_params=_cparams(("parallel",)),
        name="peer_route",
    )(h1, g, wq, sk)


def _gelu_rcp(x):
    k0 = -2.0 * math.sqrt(2.0 / math.pi)
    return x / (1.0 + jnp.exp(x * (k0 + (k0 * 0.044715) * (x * x))))


def _rows_bf16(x):
    t16 = jnp.broadcast_to(x, (16, x.shape[1])).astype(BF16)
    return jnp.concatenate([t16] * (N_KEYS // 16), axis=0)


def _peer_kernel(xn_ref, h_ref, u_ref, v_ref, e1_ref, c1_ref, e2_ref, r2_ref, o_ref, *, nb):
    e = pl.program_id(1)

    slab = h_ref.shape[1]
    nslab = o_ref.shape[1] // slab

    @pl.when(e == 0)
    def _():
        o_ref[:, 0:slab] = h_ref[...]
        o_ref[:, slab:] = jnp.zeros((o_ref.shape[0], o_ref.shape[1] - slab), F32)

    for cs in range(1, nslab):
        @pl.when(e == cs)
        def _(cs=cs):
            o_ref[:, cs * slab:(cs + 1) * slab] += h_ref[...]

    act = jnp.dot(u_ref[...], xn_ref[...], preferred_element_type=F32)
    ga = _gelu_rcp(act)
    tm = ga.shape[1]
    parts = []
    for kb in range(nb):
        n1 = e * nb + kb
        w = jnp.zeros((N_KEYS, tm), BF16)
        for h in range(PEER_HEADS):
            c1 = _rows_bf16(c1_ref[h, pl.ds(n1, 1), :])
            e1 = _rows_bf16(e1_ref[h, pl.ds(n1, 1), :])
            w = w + jnp.where(r2_ref[h] < c1, e1 * e2_ref[h], jnp.zeros_like(e1))
        parts.append((w.astype(F32) * ga[kb * N_KEYS:(kb + 1) * N_KEYS, :]).T.astype(BF16))
    wa = jnp.concatenate(parts, axis=1)
    o_ref[...] += jnp.dot(wa, v_ref[...], preferred_element_type=F32)


def _peer(xnt, h1, u, v, e1, c1, e2, r2, tm, ec):
    d, t = xnt.shape
    nchunk = u.shape[0] // ec
    nb = ec // N_KEYS
    nslab = 4
    assert nchunk >= nslab and d % (nslab * 128) == 0
    rspec = pl.BlockSpec((PEER_HEADS, N_KEYS, tm), lambda i, e: (0, 0, i))
    return pl.pallas_call(
        functools.partial(_peer_kernel, nb=nb),
        out_shape=jax.ShapeDtypeStruct((t, d), F32),
        grid=(t // tm, nchunk),
        in_specs=[pl.BlockSpec((d, tm), lambda i, e: (0, i)),
                  pl.BlockSpec((tm, d // nslab), lambda i, e: (i, jnp.minimum(e, nslab - 1))),
                  pl.BlockSpec((ec, d), lambda i, e: (e, 0)),
                  pl.BlockSpec((ec, d), lambda i, e: (e, 0)),
                  rspec, rspec, rspec, rspec],
        out_specs=pl.BlockSpec((tm, d), lambda i, e: (i, 0)),
        compiler_params=_cparams(("parallel", "arbitrary")),
        name="peer_experts",
    )(xnt, h1, u, v, e1, c1, e2, r2)


def _rope_tables(s):
    rows = s // GRID_W
    row = jnp.repeat(jnp.arange(rows, dtype=F32), GRID_W)
    col = (jnp.arange(rows * GRID_W) % GRID_W).astype(F32)
    inv = ROPE_THETA ** (-jnp.arange(ROPE_PAIRS, dtype=F32) / ROPE_PAIRS)
    ar = row[:, None] * inv[None, :]
    ac = col[:, None] * inv[None, :]
    ang = jnp.concatenate([ar, ar, ac, ac], axis=-1)
    lane = jnp.arange(HEAD_DIM)
    sign = jnp.where((lane % (2 * ROPE_PAIRS)) < ROPE_PAIRS, -1.0, 1.0).astype(F32)
    return jnp.cos(ang), jnp.sin(ang) * sign[None, :]


def _tile(n, pref):
    t = min(pref, n)
    while n % t:
        t //= 2
    return t


def kernel(x, meta_tokens, norm1_g, w_in, q_norm_g, k_norm_g, conv_w, conv_b, w_rg, b_rg, w_ig,
           b_ig, lru_lambda, attn_out_g, lru_out_g, w_out, norm2_g, peer_wq, peer_subkeys,
           peer_u, peer_v):
    b, s, d = x.shape
    t = b * s
    depth = w_in.shape[0]
    assert depth == 1, "meta rows of the stream are only materialised as attention / recurrence context"
    d_attn = attn_out_g.shape[-1]
    d_rnn = lru_out_g.shape[-1]
    d_kv = N_KV * HEAD_DIM
    off_x = d_attn + 2 * d_kv
    off_y = off_x + d_rnn
    nblk = d_rnn // LRU_BW

    x2d = x.reshape(t, d)
    meta = meta_tokens.astype(x.dtype)
    g1 = norm1_g[0].reshape(1, d)
    w_in_b = w_in[0].astype(BF16)

    tm = _tile(t, 512)
    z = _in_proj(x2d, g1, w_in_b, tm, 1024)
    zm = _in_proj(meta, g1, w_in_b, N_META, 1024)

    cos, sin_s = _rope_tables(s)
    qg = q_norm_g[0].reshape(1, HEAD_DIM)
    kg = k_norm_g[0].reshape(1, HEAD_DIM)
    q, k, v = _qk_prep(z, cos, sin_s, qg, kg, _tile(s, 256), d_attn)
    _, km, vm = _qk_prep(zm, jnp.ones((N_META, HEAD_DIM), F32), jnp.zeros((N_META, HEAD_DIM), F32),
                         qg, kg, N_META, d_attn)
    pad = ((0, N_KEYS - N_META), (0, 0))
    attn = _attention(q, k, v, jnp.pad(km, pad), jnp.pad(vm, pad), b, s, _tile(s, 1024))

    wg = jnp.concatenate([w_rg[0, 0], w_ig[0, 0], w_rg[0, 1], w_ig[0, 1]], axis=-1)
    wg = (0.5 * wg).astype(BF16)
    bg = 0.5 * jnp.stack([b_rg[0, 0], b_ig[0, 0], b_rg[0, 1], b_ig[0, 1]], axis=0)
    bg = bg.reshape(4, nblk, LRU_BW).transpose(1, 0, 2).reshape(nblk, 1, 4 * LRU_BW)
    lru = _lru(z, zm, conv_w[0], conv_b[0].reshape(1, d_rnn), wg, bg, lru_lambda[0],
               b, s, off_x, off_y, d_rnn)

    h1 = _out_proj(attn, lru, attn_out_g[0].reshape(1, d_attn), lru_out_g[0].reshape(1, d_rnn),
                   w_out[0].astype(BF16), x2d, tm, 1024)

    sk = peer_subkeys[0].reshape(2 * PEER_HEADS, N_KEYS, -1).astype(BF16)
    xn2, e1, c1, e2, r2 = _route(h1, norm2_g[0].reshape(1, d), peer_wq[0].astype(BF16), sk,
                                 _tile(t, 256))
    out = _peer(xn2.T, h1, peer_u[0].astype(BF16), peer_v[0].astype(BF16), e1, c1, e2, r2,
                _tile(t, 512), 512)
    return out.reshape(b, s, d)
```

```python
import functools
import math

import jax
import jax.numpy as jnp
from jax import lax
from jax.experimental import pallas as pl
from jax.experimental.pallas import tpu as pltpu

F32 = jnp.float32
BF16 = jnp.bfloat16
I32 = jnp.int32

EPS = 1e-6
N_META = 16
GRID_W = 64
HEAD_DIM = 128
N_KV = 4
Q_PER_KV = 4
ROPE_THETA = 10000.0
ROPE_PAIRS = HEAD_DIM // 4
LRU_BW = 128
LRU_C = 8.0
CONV_W = 4
PEER_HEADS = 8
N_KEYS = 128
PEER_TOPK = 16
ATTN_SCALE = 1.0 / math.sqrt(HEAD_DIM)
NEG_BIG = -1e30

VMEM_LIMIT = 60 * 1024 * 1024


def _cparams(sem, flags=None):
    return pltpu.CompilerParams(dimension_semantics=sem, vmem_limit_bytes=VMEM_LIMIT, flags=flags)


def _rms(x, g):
    ms = jnp.mean(x * x, axis=-1, keepdims=True)
    return x * lax.rsqrt(ms + EPS) * g


def _gelu(x):
    c = math.sqrt(2.0 / math.pi)
    return 0.5 * x * (1.0 + jnp.tanh(c * (x + 0.044715 * (x * x * x))))


def _dot_nt(a, b):
    return lax.dot_general(a, b, (((1,), (1,)), ((), ())), preferred_element_type=F32)


def _in_proj_kernel(x_ref, g_ref, w_ref, o_ref, xn_ref):
    @pl.when(pl.program_id(1) == 0)
    def _():
        xn_ref[...] = _rms(x_ref[...], g_ref[...]).astype(BF16)

    o_ref[...] = jnp.dot(xn_ref[...], w_ref[...], preferred_element_type=F32)


def _in_proj(x2d, g, w, tm, tn):
    t, d = x2d.shape
    n = w.shape[1]
    return pl.pallas_call(
        _in_proj_kernel,
        out_shape=jax.ShapeDtypeStruct((t, n), F32),
        grid=(t // tm, n // tn),
        in_specs=[pl.BlockSpec((tm, d), lambda i, j: (i, 0)),
                  pl.BlockSpec((1, d), lambda i, j: (0, 0)),
                  pl.BlockSpec((d, tn), lambda i, j: (0, j))],
        out_specs=pl.BlockSpec((tm, tn), lambda i, j: (i, j)),
        scratch_shapes=[pltpu.VMEM((tm, d), BF16)],
        compiler_params=_cparams(("parallel", "arbitrary")),
        name="in_proj",
    )(x2d, g, w)


def _qk_prep_kernel(zqk_ref, zv_ref, cos_ref, sin_ref, qg_ref, kg_ref,
                    q_ref, k_ref, v_ref, *, n_q):
    cos = cos_ref[...]
    sin = sin_ref[...]
    lane = lax.broadcasted_iota(I32, cos.shape, 1)
    lo = (lane % (2 * ROPE_PAIRS)) < ROPE_PAIRS

    def norm_rope(t, g):
        tn = _rms(t, g)
        rot = jnp.where(lo, pltpu.roll(tn, HEAD_DIM - ROPE_PAIRS, 1), pltpu.roll(tn, ROPE_PAIRS, 1))
        return (tn * cos + rot * sin).astype(BF16)

    for h in range(n_q):
        sl = slice(h * HEAD_DIM, (h + 1) * HEAD_DIM)
        q_ref[:, sl] = norm_rope(zqk_ref[:, sl], qg_ref[...])
    for h in range(N_KV):
        src = slice((n_q + h) * HEAD_DIM, (n_q + h + 1) * HEAD_DIM)
        k_ref[:, h * HEAD_DIM:(h + 1) * HEAD_DIM] = norm_rope(zqk_ref[:, src], kg_ref[...])
    v_ref[...] = zv_ref[...].astype(BF16)


def _qk_prep(z, cos, sin_signed, qg, kg, tm, d_attn):
    t = z.shape[0]
    n_q = d_attn // HEAD_DIM
    d_kv = N_KV * HEAD_DIM
    w_qk = d_attn + d_kv
    n_tab = cos.shape[0] // tm
    return pl.pallas_call(
        functools.partial(_qk_prep_kernel, n_q=n_q),
        out_shape=(jax.ShapeDtypeStruct((t, d_attn), BF16),
                   jax.ShapeDtypeStruct((t, d_kv), BF16),
                   jax.ShapeDtypeStruct((t, d_kv), BF16)),
        grid=(t // tm,),
        in_specs=[pl.BlockSpec((tm, w_qk), lambda i: (i, 0)),
                  pl.BlockSpec((tm, d_kv), lambda i: (i, w_qk // d_kv)),
                  pl.BlockSpec((tm, HEAD_DIM), lambda i: (i % n_tab, 0)),
                  pl.BlockSpec((tm, HEAD_DIM), lambda i: (i % n_tab, 0)),
                  pl.BlockSpec((1, HEAD_DIM), lambda i: (0, 0)),
                  pl.BlockSpec((1, HEAD_DIM), lambda i: (0, 0))],
        out_specs=(pl.BlockSpec((tm, d_attn), lambda i: (i, 0)),
                   pl.BlockSpec((tm, d_kv), lambda i: (i, 0)),
                   pl.BlockSpec((tm, d_kv), lambda i: (i, 0))),
        compiler_params=_cparams(("parallel",)),
        name="qk_prep",
    )(z, z, cos, sin_signed, qg, kg)


def _attn_head(q, k, v, km, vm, meta_bias, mc):
    c = ATTN_SCALE * math.log2(math.e)
    ns = k.shape[0]
    kc = min(ns, 512)
    lw = km.shape[0]
    sm = _dot_nt(q, km) + meta_bias
    if mc is None:
        s = _dot_nt(q, k)
        chunks = [s[:, j * kc:(j + 1) * kc] for j in range(ns // kc)]
        mx = sm
        for j in range(ns // lw):
            mx = jnp.maximum(mx, s[:, j * lw:(j + 1) * lw])
        mc = jnp.max(mx, axis=-1, keepdims=True) * c
    else:
        chunks = [_dot_nt(q, k[j * kc:(j + 1) * kc, :]) for j in range(ns // kc)]
    pm = jnp.exp2(sm * c - mc)
    lsum = pm
    o = jnp.dot(pm.astype(BF16), vm, preferred_element_type=F32)
    for j in range(ns // kc):
        parts = []
        for i in range(kc // lw):
            p = jnp.exp2(chunks[j][:, i * lw:(i + 1) * lw] * c - mc)
            lsum = lsum + p
            parts.append(p.astype(BF16))
        o = o + jnp.dot(jnp.concatenate(parts, axis=1), v[j * kc:(j + 1) * kc, :],
                        preferred_element_type=F32)
    return o, jnp.sum(lsum, axis=-1, keepdims=True)


ATTN_MIN_ROW_SUM = 2.0 ** -60


def _attn_kernel(q_ref, k_ref, v_ref, km_ref, vm_ref, o_ref, kmax_scr):
    k = k_ref[...]
    v = v_ref[...]
    km = km_ref[...]
    vm = vm_ref[...]
    col = lax.broadcasted_iota(I32, (1, km.shape[0]), 1)
    meta_bias = jnp.where(col < N_META, 0.0, NEG_BIG).astype(F32)
    c = ATTN_SCALE * math.log2(math.e)

    @pl.when(pl.program_id(2) == 0)
    def _():
        kf = k.astype(F32)
        kmf = km.astype(F32)
        n2 = jnp.maximum(jnp.max(jnp.sum(kf * kf, axis=-1, keepdims=True), axis=0, keepdims=True),
                         jnp.max(jnp.sum(kmf * kmf, axis=-1, keepdims=True), axis=0, keepdims=True))
        kmax_scr[...] = jnp.broadcast_to(jnp.sqrt(n2), kmax_scr.shape)

    kmax_c = kmax_scr[:, 0:1] * c
    lmin = None
    for g in range(Q_PER_KV):
        sl = slice(g * HEAD_DIM, (g + 1) * HEAD_DIM)
        q = q_ref[:, sl]
        qf = q.astype(F32)
        mc = jnp.sqrt(jnp.sum(qf * qf, axis=-1, keepdims=True)) * kmax_c
        o, l = _attn_head(q, k, v, km, vm, meta_bias, mc)
        o_ref[:, sl] = o / l
        lg = jnp.min(l)
        lmin = lg if lmin is None else jnp.minimum(lmin, lg)

    @pl.when(jnp.logical_not(lmin >= ATTN_MIN_ROW_SUM))
    def _():
        for g in range(Q_PER_KV):
            sl = slice(g * HEAD_DIM, (g + 1) * HEAD_DIM)
            o, l = _attn_head(q_ref[:, sl], k, v, km, vm, meta_bias, None)
            o_ref[:, sl] = o / l


def _attention(q, k, v, km, vm, b, s, tq):
    t, d_attn = q.shape
    wq = Q_PER_KV * HEAD_DIM
    nq = s // tq
    mp = km.shape[0]
    return pl.pallas_call(
        _attn_kernel,
        out_shape=jax.ShapeDtypeStruct((t, d_attn), F32),
        grid=(b, N_KV, nq),
        in_specs=[pl.BlockSpec((tq, wq), lambda bi, kh, qi: (bi * nq + qi, kh)),
                  pl.BlockSpec((s, HEAD_DIM), lambda bi, kh, qi: (bi, kh)),
                  pl.BlockSpec((s, HEAD_DIM), lambda bi, kh, qi: (bi, kh)),
                  pl.BlockSpec((mp, HEAD_DIM), lambda bi, kh, qi: (0, kh)),
                  pl.BlockSpec((mp, HEAD_DIM), lambda bi, kh, qi: (0, kh))],
        out_specs=pl.BlockSpec((tq, wq), lambda bi, kh, qi: (bi * nq + qi, kh)),
        scratch_shapes=[pltpu.VMEM((1, HEAD_DIM), F32)],
        compiler_params=_cparams(("parallel", "parallel", "arbitrary")),
        name="attention",
    )(q, k, v, km, vm)


def _lru_kernel(xr_ref, yr_ref, xm_ref, cw_ref, cb_ref, wg_ref, bg_ref, lam_ref, uf_ref, vf_ref,
                o_ref, ub_ref, vb_ref, xpad, af, bf, ab, bb, hfs, pfs, hbs, pbs, *, s, rc):
    l = s + N_META
    bw = LRU_BW
    ub_ref[...] = uf_ref[...].astype(BF16)
    vb_ref[...] = vf_ref[...].astype(BF16)
    xpad[0:8, :] = jnp.zeros((8, bw), F32)
    xpad[8:8 + N_META, :] = xm_ref[...]
    xpad[8 + N_META:8 + l, :] = xr_ref[...]
    xpad[8 + l:16 + l, :] = jnp.zeros((8, bw), F32)

    lam = lam_ref[...]
    neg = -lam
    sp = jnp.maximum(neg, 0.0) + jnp.log1p(jnp.exp(-jnp.abs(neg)))
    cw = cw_ref[...]
    cb = cb_ref[...]
    wg = wg_ref[...]
    bg = bg_ref[...]

    def gates(t0, n):
        xc = (cw[0:1] * xpad[pl.ds(t0 + 6, n), :] + cw[1:2] * xpad[pl.ds(t0 + 7, n), :]
              + cw[2:3] * xpad[pl.ds(t0 + 8, n), :] + cw[3:4] * xpad[pl.ds(t0 + 9, n), :]) + cb
        gt = jnp.dot(xc.astype(BF16), wg, preferred_element_type=F32) + bg
        hx = 0.5 * xc
        for d, (a_ref, b_ref) in enumerate(((af, bf), (ab, bb))):
            tr = jnp.tanh(gt[:, (2 * d) * bw:(2 * d + 1) * bw])
            ti = jnp.tanh(gt[:, (2 * d + 1) * bw:(2 * d + 2) * bw])
            hc = (-0.5 * LRU_C) * sp[d:d + 1]
            log_a = hc + hc * tr
            a = jnp.exp(log_a)
            om = 1.0 - a * a
            root = jnp.where(om > 0.0, om * lax.rsqrt(om), 0.0)
            a_ref[pl.ds(t0, n), :] = a
            b_ref[pl.ds(t0, n), :] = root * (hx + hx * ti)

    gates(0, N_META)

    def gate_body(c, carry):
        gates(pl.multiple_of(N_META + c * rc, 8), rc)
        return carry

    lax.fori_loop(0, s // rc, gate_body, 0)

    nch = 8
    cl = l // nch

    def rows(ref, i):
        return ref.at[pl.ds(i, nch, stride=cl), :]

    def pass1(i, carry):
        hf, pf, hb, pb = carry
        a = rows(af, i)[...]
        hf = a * hf + rows(bf, i)[...]
        pf = a * pf
        rows(hfs, i)[...] = hf
        rows(pfs, i)[...] = pf
        j = cl - 1 - i
        a = rows(ab, j)[...]
        hb = a * hb + rows(bb, j)[...]
        pb = a * pb
        rows(hbs, j)[...] = hb
        rows(pbs, j)[...] = pb
        return hf, pf, hb, pb

    zero = jnp.zeros((nch, bw), F32)
    one = jnp.ones((nch, bw), F32)
    hf, pf, hb, pb = lax.fori_loop(0, cl, pass1, (zero, one, zero, one), unroll=8)

    cf = [jnp.zeros((1, bw), F32)]
    for c in range(1, nch):
        cf.append(hf[c - 1:c] + pf[c - 1:c] * cf[-1])
    cin_f = jnp.concatenate(cf, axis=0)
    cbk = [jnp.zeros((1, bw), F32)]
    for c in range(nch - 2, -1, -1):
        cbk.append(hb[c + 1:c + 2] + pb[c + 1:c + 2] * cbk[-1])
    cin_b = jnp.concatenate(cbk[::-1], axis=0)

    def pass2(i, carry):
        rows(hfs, i)[...] = rows(hfs, i)[...] + rows(pfs, i)[...] * cin_f
        rows(hbs, i)[...] = rows(hbs, i)[...] + rows(pbs, i)[...] * cin_b
        return carry

    lax.fori_loop(0, cl, pass2, 0, unroll=8)

    for c in range(s // rc):
        r0 = c * rc
        hsum = hfs[N_META + r0:N_META + r0 + rc, :] + hbs[N_META + r0:N_META + r0 + rc, :]
        o_ref[r0:r0 + rc, :] = hsum * _gelu(yr_ref[r0:r0 + rc, :])


def _lru(z, zm, cw, cb, wg, bg, lam, uf, vf, b, s, off_x, off_y, d_rnn):
    nblk = d_rnn // LRU_BW
    ne, dm = uf.shape
    assert ne % (b * nblk) == 0
    er = ne // (b * nblk)
    espec = pl.BlockSpec((er, dm), lambda bi, n: (bi * nblk + n, 0))
    rc = min(256, s)
    l = s + N_META
    bx = off_x // LRU_BW
    by = off_y // LRU_BW
    return pl.pallas_call(
        functools.partial(_lru_kernel, s=s, rc=rc),
        out_shape=(jax.ShapeDtypeStruct((b * s, d_rnn), F32),
                   jax.ShapeDtypeStruct((ne, dm), BF16), jax.ShapeDtypeStruct((ne, dm), BF16)),
        grid=(b, nblk),
        in_specs=[pl.BlockSpec((s, LRU_BW), lambda bi, n: (bi, bx + n)),
                  pl.BlockSpec((s, LRU_BW), lambda bi, n: (bi, by + n)),
                  pl.BlockSpec((N_META, LRU_BW), lambda bi, n: (0, bx + n)),
                  pl.BlockSpec((CONV_W, LRU_BW), lambda bi, n: (0, n)),
                  pl.BlockSpec((1, LRU_BW), lambda bi, n: (0, n)),
                  pl.BlockSpec((None, LRU_BW, 4 * LRU_BW), lambda bi, n: (n, 0, 0)),
                  pl.BlockSpec((None, 1, 4 * LRU_BW), lambda bi, n: (n, 0, 0)),
                  pl.BlockSpec((2, LRU_BW), lambda bi, n: (0, n)),
                  espec, espec],
        out_specs=(pl.BlockSpec((s, LRU_BW), lambda bi, n: (bi, n)), espec, espec),
        scratch_shapes=[pltpu.VMEM((l + 16, LRU_BW), F32)] + [pltpu.VMEM((l, LRU_BW), F32)] * 8,
        compiler_params=_cparams(("parallel", "parallel")),
        name="rg_lru",
    )(z, z, zm, cw, cb, wg, bg, lam, uf, vf)


def _out_proj_kernel(a_ref, r_ref, ga_ref, gr_ref, w_ref, x_ref, o_ref, m_ref):
    da = a_ref.shape[1]

    @pl.when(pl.program_id(1) == 0)
    def _():
        m_ref[:, 0:da] = _rms(a_ref[...], ga_ref[...]).astype(BF16)
        m_ref[:, da:] = _rms(r_ref[...], gr_ref[...]).astype(BF16)

    o_ref[...] = x_ref[...] + jnp.dot(m_ref[...], w_ref[...], preferred_element_type=F32)


def _out_proj(attn, lru, ga, gr, w, x2d, tm, tn):
    t, da = attn.shape
    dr = lru.shape[1]
    d = w.shape[1]
    return pl.pallas_call(
        _out_proj_kernel,
        out_shape=jax.ShapeDtypeStruct((t, d), F32),
        grid=(t // tm, d // tn),
        in_specs=[pl.BlockSpec((tm, da), lambda i, j: (i, 0)),
                  pl.BlockSpec((tm, dr), lambda i, j: (i, 0)),
                  pl.BlockSpec((1, da), lambda i, j: (0, 0)),
                  pl.BlockSpec((1, dr), lambda i, j: (0, 0)),
                  pl.BlockSpec((da + dr, tn), lambda i, j: (0, j)),
                  pl.BlockSpec((tm, tn), lambda i, j: (i, j))],
        out_specs=pl.BlockSpec((tm, tn), lambda i, j: (i, j)),
        scratch_shapes=[pltpu.VMEM((tm, da + dr), BF16)],
        compiler_params=_cparams(("parallel", "arbitrary")),
        name="out_proj",
    )(attn, lru, ga, gr, w, x2d)


def _topk_rows(sc, k):
    n, tm = sc.shape
    row = lax.broadcasted_iota(I32, (n, tm), 0)
    cur = sc
    rank = jnp.full((n, tm), k, I32)
    vals = []
    for r in range(k):
        mx = jnp.max(cur, axis=0, keepdims=True)
        idx = jnp.min(jnp.where(cur == mx, row, n), axis=0, keepdims=True)
        sel = row == idx
        rank = jnp.where(sel, r, rank)
        cur = jnp.where(sel, -jnp.inf, cur)
        vals.append(mx)
    return jnp.concatenate(vals, axis=0), rank


def _topk_rows_distinct(sc, k):
    cur = sc
    rank = jnp.full(sc.shape, k, I32)
    vals = []
    for r in range(k):
        mx = jnp.max(cur, axis=0, keepdims=True)
        sel = cur == mx
        rank = jnp.where(sel, r, rank)
        cur = jnp.where(sel, -jnp.inf, cur)
        vals.append(mx)
    return jnp.concatenate(vals, axis=0), rank


def _count_rows(x):
    return jnp.sum(x.astype(I32), axis=0, keepdims=True)


def _route_kernel(h_ref, g_ref, wq_ref, sk_ref, xn_ref, e1_ref, c1_ref, e2_ref, r2_ref,
                  q_scr, r1_s, r2_s, cnt_s, z_s):
    k = PEER_TOPK
    xn = _rms(h_ref[...], g_ref[...]).astype(BF16)
    xn_ref[...] = xn
    q = jnp.dot(xn, wq_ref[...], preferred_element_type=F32).astype(BF16)
    tm = q.shape[0]
    for c in range(2 * PEER_HEADS):
        q_scr[c] = q[:, c * N_KEYS:(c + 1) * N_KEYS]

    hk = k // 2
    pos_col = jnp.concatenate(
        [lax.broadcasted_iota(I32, (k, 1), 0)]
        + [a * k + lax.broadcasted_iota(I32, (hk, 1), 0) for a in range(1, k)], axis=0)
    nc = pos_col.shape[0]
    arow = lax.broadcasted_iota(I32, (k, tm), 0)

    def candidates(v1, v2):
        return jnp.concatenate(
            [v1[0:1] + v2] + [v1[a:a + 1] + v2[0:hk] for a in range(1, k)], axis=0)

    def head_body(h, carry):
        s1 = _dot_nt(sk_ref[2 * h], q_scr[2 * h])
        s2 = _dot_nt(sk_ref[2 * h + 1], q_scr[2 * h + 1])

        v1, r1 = _topk_rows_distinct(s1, k)
        v2, r2 = _topk_rows_distinct(s2, k)
        cand = candidates(v1, v2)
        cur = cand
        top = v1[0:1] + v2[0:1]
        zsum = jnp.zeros((1, tm), F32)
        mx = top
        for j in range(k):
            mx = jnp.max(cur, axis=0, keepdims=True)
            cur = jnp.where(cur == mx, -jnp.inf, cur)
            zsum = zsum + jnp.exp(mx - top)
        selc = cand >= mx
        cnt = jnp.concatenate(
            [_count_rows(selc[0:k])]
            + [_count_rows(selc[k + hk * (a - 1):k + hk * a]) for a in range(1, k)], axis=0)
        bad = jnp.logical_or(
            jnp.logical_or(_count_rows(r1 < k) != k, _count_rows(r2 < k) != k),
            _count_rows(cnt) != k)
        r1_s[...] = r1
        r2_s[...] = r2
        cnt_s[...] = cnt
        z_s[...] = zsum

        @pl.when(jnp.max(bad.astype(I32)) > 0)
        def _():
            v1x, r1x = _topk_rows(s1, k)
            v2x, r2x = _topk_rows(s2, k)
            pos = jnp.broadcast_to(pos_col, (nc, tm))
            curx = candidates(v1x, v2x)
            mask_a = jnp.zeros((k, tm), I32)
            zx = jnp.zeros((1, tm), F32)
            for j in range(k):
                mxx = jnp.max(curx, axis=0, keepdims=True)
                pj = jnp.min(jnp.where(curx == mxx, pos, k * k), axis=0, keepdims=True)
                curx = jnp.where(pos == pj, -jnp.inf, curx)
                zx = zx + jnp.exp(mxx - top)
                aj = jnp.right_shift(pj, 4)
                bj = jnp.bitwise_and(pj, k - 1)
                mask_a = jnp.where(arow == aj, jnp.bitwise_or(mask_a, jnp.left_shift(1, bj)), mask_a)
            r1_s[...] = r1x
            r2_s[...] = r2x
            cnt_s[...] = lax.population_count(mask_a)
            z_s[...] = zx

        r1f = r1_s[...]
        cntf = cnt_s[...]
        c1 = jnp.zeros(r1f.shape, I32)
        for a in range(k):
            c1 = jnp.where(r1f == a, cntf[a:a + 1], c1)
        e1_ref[h] = jnp.exp(s1 - v1[0:1] - jnp.log(z_s[...]))
        c1_ref[h] = c1.astype(F32)
        e2_ref[h] = jnp.exp(s2 - v2[0:1]).astype(BF16)
        r2_ref[h] = r2_s[...].astype(F32).astype(BF16)
        return carry

    lax.fori_loop(0, PEER_HEADS, head_body, 0)


def _route(h1, g, wq, sk, tm):
    t, d = h1.shape
    nq = wq.shape[1]
    rf = jax.ShapeDtypeStruct((PEER_HEADS, N_KEYS, t), F32)
    rb = jax.ShapeDtypeStruct((PEER_HEADS, N_KEYS, t), BF16)
    rspec = pl.BlockSpec((PEER_HEADS, N_KEYS, tm), lambda i: (0, 0, i))
    return pl.pallas_call(
        _route_kernel,
        out_shape=(jax.ShapeDtypeStruct((t, d), BF16), rf, rf, rb, rb),
        grid=(t // tm,),
        in_specs=[pl.BlockSpec((tm, d), lambda i: (i, 0)),
                  pl.BlockSpec((1, d), lambda i: (0, 0)),
                  pl.BlockSpec((d, nq), lambda i: (0, 0), pipeline_mode=pl.Buffered(1)),
                  pl.BlockSpec((2 * PEER_HEADS, N_KEYS, N_KEYS), lambda i: (0, 0, 0))],
        out_specs=(pl.BlockSpec((tm, d), lambda i: (i, 0)), rspec, rspec, rspec, rspec),
        scratch_shapes=[pltpu.VMEM((2 * PEER_HEADS, tm, N_KEYS), BF16),
                        pltpu.VMEM((N_KEYS, tm), I32), pltpu.VMEM((N_KEYS, tm), I32),
                        pltpu.VMEM((PEER_TOPK, tm), I32), pltpu.VMEM((1, tm), F32)],
        compiler_params=_cparams(("parallel",)),
        name="peer_route",
    )(h1, g, wq, sk)


def _gelu_rcp(x):
    k0 = -2.0 * math.sqrt(2.0 / math.pi)
    return x / (1.0 + jnp.exp(x * (k0 + (k0 * 0.044715) * (x * x))))


def _rows_bf16(x):
    t16 = jnp.broadcast_to(x, (16, x.shape[1])).astype(BF16)
    return jnp.concatenate([t16] * (N_KEYS // 16), axis=0)


def _peer_kernel(xn_ref, h_ref, u_ref, v_ref, e1_ref, c1_ref, e2_ref, r2_ref, o_ref, *, nb):
    e = pl.program_id(1)

    slab = h_ref.shape[1]
    nslab = o_ref.shape[1] // slab

    @pl.when(e == 0)
    def _():
        o_ref[:, 0:slab] = h_ref[...]
        o_ref[:, slab:] = jnp.zeros((o_ref.shape[0], o_ref.shape[1] - slab), F32)

    for cs in range(1, nslab):
        @pl.when(e == cs)
        def _(cs=cs):
            o_ref[:, cs * slab:(cs + 1) * slab] += h_ref[...]

    act = jnp.dot(u_ref[...], xn_ref[...], preferred_element_type=F32)
    ga = _gelu_rcp(act)
    tm = ga.shape[1]
    parts = []
    for kb in range(nb):
        n1 = e * nb + kb
        w = jnp.zeros((N_KEYS, tm), BF16)
        for h in range(PEER_HEADS):
            c1 = _rows_bf16(c1_ref[h, pl.ds(n1, 1), :])
            e1 = _rows_bf16(e1_ref[h, pl.ds(n1, 1), :])
            w = w + jnp.where(r2_ref[h] < c1, e1 * e2_ref[h], jnp.zeros_like(e1))
        parts.append((w.astype(F32) * ga[kb * N_KEYS:(kb + 1) * N_KEYS, :]).T.astype(BF16))
    wa = jnp.concatenate(parts, axis=1)
    o_ref[...] += jnp.dot(wa, v_ref[...], preferred_element_type=F32)


def _peer(xnt, h1, u, v, e1, c1, e2, r2, tm, ec):
    d, t = xnt.shape
    nchunk = u.shape[0] // ec
    nb = ec // N_KEYS
    nslab = 4
    assert nchunk >= nslab and d % (nslab * 128) == 0
    rspec = pl.BlockSpec((PEER_HEADS, N_KEYS, tm), lambda i, e: (0, 0, i))
    return pl.pallas_call(
        functools.partial(_peer_kernel, nb=nb),
        out_shape=jax.ShapeDtypeStruct((t, d), F32),
        grid=(t // tm, nchunk),
        in_specs=[pl.BlockSpec((d, tm), lambda i, e: (0, i)),
                  pl.BlockSpec((tm, d // nslab), lambda i, e: (i, jnp.minimum(e, nslab - 1))),
                  pl.BlockSpec((ec, d), lambda i, e: (e, 0)),
                  pl.BlockSpec((ec, d), lambda i, e: (e, 0)),
                  rspec, rspec, rspec, rspec],
        out_specs=pl.BlockSpec((tm, d), lambda i, e: (i, 0)),
        compiler_params=_cparams(("parallel", "arbitrary")),
        name="peer_experts",
    )(xnt, h1, u, v, e1, c1, e2, r2)


def _rope_tables(s):
    rows = s // GRID_W
    row = jnp.repeat(jnp.arange(rows, dtype=F32), GRID_W)
    col = (jnp.arange(rows * GRID_W) % GRID_W).astype(F32)
    inv = ROPE_THETA ** (-jnp.arange(ROPE_PAIRS, dtype=F32) / ROPE_PAIRS)
    ar = row[:, None] * inv[None, :]
    ac = col[:, None] * inv[None, :]
    ang = jnp.concatenate([ar, ar, ac, ac], axis=-1)
    lane = jnp.arange(HEAD_DIM)
    sign = jnp.where((lane % (2 * ROPE_PAIRS)) < ROPE_PAIRS, -1.0, 1.0).astype(F32)
    return jnp.cos(ang), jnp.sin(ang) * sign[None, :]


def _tile(n, pref):
    t = min(pref, n)
    while n % t:
        t //= 2
    return t


def kernel(x, meta_tokens, norm1_g, w_in, q_norm_g, k_norm_g, conv_w, conv_b, w_rg, b_rg, w_ig,
           b_ig, lru_lambda, attn_out_g, lru_out_g, w_out, norm2_g, peer_wq, peer_subkeys,
           peer_u, peer_v):
    b, s, d = x.shape
    t = b * s
    depth = w_in.shape[0]
    assert depth == 1, "meta rows of the stream are only materialised as attention / recurrence context"
    d_attn = attn_out_g.shape[-1]
    d_rnn = lru_out_g.shape[-1]
    d_kv = N_KV * HEAD_DIM
    off_x = d_attn + 2 * d_kv
    off_y = off_x + d_rnn
    nblk = d_rnn // LRU_BW

    x2d = x.reshape(t, d)
    meta = meta_tokens.astype(x.dtype)
    g1 = norm1_g[0].reshape(1, d)
    w_in_b = w_in[0].astype(BF16)

    tm = _tile(t, 512)
    z = _in_proj(x2d, g1, w_in_b, tm, 1024)
    zm = _in_proj(meta, g1, w_in_b, N_META, 1024)

    cos, sin_s = _rope_tables(s)
    qg = q_norm_g[0].reshape(1, HEAD_DIM)
    kg = k_norm_g[0].reshape(1, HEAD_DIM)
    q, k, v = _qk_prep(z, cos, sin_s, qg, kg, _tile(s, 256), d_attn)
    _, km, vm = _qk_prep(zm, jnp.ones((N_META, HEAD_DIM), F32), jnp.zeros((N_META, HEAD_DIM), F32),
                         qg, kg, N_META, d_attn)
    pad = ((0, N_KEYS - N_META), (0, 0))
    attn = _attention(q, k, v, jnp.pad(km, pad), jnp.pad(vm, pad), b, s, _tile(s, 1024))

    wg = jnp.concatenate([w_rg[0, 0], w_ig[0, 0], w_rg[0, 1], w_ig[0, 1]], axis=-1)
    wg = (0.5 * wg).astype(BF16)
    bg = 0.5 * jnp.stack([b_rg[0, 0], b_ig[0, 0], b_rg[0, 1], b_ig[0, 1]], axis=0)
    bg = bg.reshape(4, nblk, LRU_BW).transpose(1, 0, 2).reshape(nblk, 1, 4 * LRU_BW)
    lru, u_b, v_b = _lru(z, zm, conv_w[0], conv_b[0].reshape(1, d_rnn), wg, bg, lru_lambda[0],
                         peer_u[0], peer_v[0], b, s, off_x, off_y, d_rnn)

    h1 = _out_proj(attn, lru, attn_out_g[0].reshape(1, d_attn), lru_out_g[0].reshape(1, d_rnn),
                   w_out[0].astype(BF16), x2d, tm, 1024)

    sk = peer_subkeys[0].reshape(2 * PEER_HEADS, N_KEYS, -1).astype(BF16)
    xn2, e1, c1, e2, r2 = _route(h1, norm2_g[0].reshape(1, d), peer_wq[0].astype(BF16), sk,
                                 _tile(t, 256))
    out = _peer(xn2.T, h1, u_b, v_b, e1, c1, e2, r2,
                _tile(t, 512), 512)
    return out.reshape(b, s, d)
```

```python
import functools
import math

import jax
import jax.numpy as jnp
from jax import lax
from jax.experimental import pallas as pl
from jax.experimental.pallas import tpu as pltpu

F32 = jnp.float32
BF16 = jnp.bfloat16
I32 = jnp.int32

EPS = 1e-6
N_META = 16
GRID_W = 64
HEAD_DIM = 128
N_KV = 4
Q_PER_KV = 4
ROPE_THETA = 10000.0
ROPE_PAIRS = HEAD_DIM // 4
LRU_BW = 128
LRU_C = 8.0
CONV_W = 4
PEER_HEADS = 8
N_KEYS = 128
PEER_TOPK = 16
ATTN_SCALE = 1.0 / math.sqrt(HEAD_DIM)
NEG_BIG = -1e30

VMEM_LIMIT = 60 * 1024 * 1024


def _cparams(sem, flags=None):
    return pltpu.CompilerParams(dimension_semantics=sem, vmem_limit_bytes=VMEM_LIMIT, flags=flags)


def _rms(x, g):
    ms = jnp.mean(x * x, axis=-1, keepdims=True)
    return x * lax.rsqrt(ms + EPS) * g


def _gelu(x):
    c = math.sqrt(2.0 / math.pi)
    return 0.5 * x * (1.0 + jnp.tanh(c * (x + 0.044715 * (x * x * x))))


def _dot_nt(a, b):
    return lax.dot_general(a, b, (((1,), (1,)), ((), ())), preferred_element_type=F32)


def _in_proj_kernel(x_ref, g_ref, w_ref, o_ref, xn_ref):
    @pl.when(pl.program_id(1) == 0)
    def _():
        xn_ref[...] = _rms(x_ref[...], g_ref[...]).astype(BF16)

    o_ref[...] = jnp.dot(xn_ref[...], w_ref[...], preferred_element_type=F32)


def _in_proj(x2d, g, w, tm, tn):
    t, d = x2d.shape
    n = w.shape[1]
    return pl.pallas_call(
        _in_proj_kernel,
        out_shape=jax.ShapeDtypeStruct((t, n), F32),
        grid=(t // tm, n // tn),
        in_specs=[pl.BlockSpec((tm, d), lambda i, j: (i, 0)),
                  pl.BlockSpec((1, d), lambda i, j: (0, 0)),
                  pl.BlockSpec((d, tn), lambda i, j: (0, j))],
        out_specs=pl.BlockSpec((tm, tn), lambda i, j: (i, j)),
        scratch_shapes=[pltpu.VMEM((tm, d), BF16)],
        compiler_params=_cparams(("parallel", "arbitrary")),
        name="in_proj",
    )(x2d, g, w)


def _qk_prep_kernel(zqk_ref, zv_ref, cos_ref, sin_ref, qg_ref, kg_ref,
                    q_ref, k_ref, v_ref, *, n_q):
    cos = cos_ref[...]
    sin = sin_ref[...]
    lane = lax.broadcasted_iota(I32, cos.shape, 1)
    lo = (lane % (2 * ROPE_PAIRS)) < ROPE_PAIRS

    def norm_rope(t, g):
        tn = _rms(t, g)
        rot = jnp.where(lo, pltpu.roll(tn, HEAD_DIM - ROPE_PAIRS, 1), pltpu.roll(tn, ROPE_PAIRS, 1))
        return (tn * cos + rot * sin).astype(BF16)

    for h in range(n_q):
        sl = slice(h * HEAD_DIM, (h + 1) * HEAD_DIM)
        q_ref[:, sl] = norm_rope(zqk_ref[:, sl], qg_ref[...])
    for h in range(N_KV):
        src = slice((n_q + h) * HEAD_DIM, (n_q + h + 1) * HEAD_DIM)
        k_ref[:, h * HEAD_DIM:(h + 1) * HEAD_DIM] = norm_rope(zqk_ref[:, src], kg_ref[...])
    v_ref[...] = zv_ref[...].astype(BF16)


def _qk_prep(z, cos, sin_signed, qg, kg, tm, d_attn):
    t = z.shape[0]
    n_q = d_attn // HEAD_DIM
    d_kv = N_KV * HEAD_DIM
    w_qk = d_attn + d_kv
    n_tab = cos.shape[0] // tm
    return pl.pallas_call(
        functools.partial(_qk_prep_kernel, n_q=n_q),
        out_shape=(jax.ShapeDtypeStruct((t, d_attn), BF16),
                   jax.ShapeDtypeStruct((t, d_kv), BF16),
                   jax.ShapeDtypeStruct((t, d_kv), BF16)),
        grid=(t // tm,),
        in_specs=[pl.BlockSpec((tm, w_qk), lambda i: (i, 0)),
                  pl.BlockSpec((tm, d_kv), lambda i: (i, w_qk // d_kv)),
                  pl.BlockSpec((tm, HEAD_DIM), lambda i: (i % n_tab, 0)),
                  pl.BlockSpec((tm, HEAD_DIM), lambda i: (i % n_tab, 0)),
                  pl.BlockSpec((1, HEAD_DIM), lambda i: (0, 0)),
                  pl.BlockSpec((1, HEAD_DIM), lambda i: (0, 0))],
        out_specs=(pl.BlockSpec((tm, d_attn), lambda i: (i, 0)),
                   pl.BlockSpec((tm, d_kv), lambda i: (i, 0)),
                   pl.BlockSpec((tm, d_kv), lambda i: (i, 0))),
        compiler_params=_cparams(("parallel",)),
        name="qk_prep",
    )(z, z, cos, sin_signed, qg, kg)


def _attn_head(q, k, v, km, vm, meta_bias, mc):
    c = ATTN_SCALE * math.log2(math.e)
    ns = k.shape[0]
    kc = min(ns, 512)
    lw = km.shape[0]
    sm = _dot_nt(q, km) + meta_bias
    if mc is None:
        s = _dot_nt(q, k)
        chunks = [s[:, j * kc:(j + 1) * kc] for j in range(ns // kc)]
        mx = sm
        for j in range(ns // lw):
            mx = jnp.maximum(mx, s[:, j * lw:(j + 1) * lw])
        mc = jnp.max(mx, axis=-1, keepdims=True) * c
    else:
        chunks = [_dot_nt(q, k[j * kc:(j + 1) * kc, :]) for j in range(ns // kc)]
    pm = jnp.exp2(sm * c - mc)
    lsum = pm
    o = jnp.dot(pm.astype(BF16), vm, preferred_element_type=F32)
    for j in range(ns // kc):
        parts = []
        for i in range(kc // lw):
            p = jnp.exp2(chunks[j][:, i * lw:(i + 1) * lw] * c - mc)
            lsum = lsum + p
            parts.append(p.astype(BF16))
        o = o + jnp.dot(jnp.concatenate(parts, axis=1), v[j * kc:(j + 1) * kc, :],
                        preferred_element_type=F32)
    return o, jnp.sum(lsum, axis=-1, keepdims=True)


ATTN_MIN_ROW_SUM = 2.0 ** -60


def _attn_kernel(q_ref, k_ref, v_ref, km_ref, vm_ref, o_ref, kmax_scr):
    k = k_ref[...]
    v = v_ref[...]
    km = km_ref[...]
    vm = vm_ref[...]
    col = lax.broadcasted_iota(I32, (1, km.shape[0]), 1)
    meta_bias = jnp.where(col < N_META, 0.0, NEG_BIG).astype(F32)
    c = ATTN_SCALE * math.log2(math.e)

    @pl.when(pl.program_id(2) == 0)
    def _():
        kf = k.astype(F32)
        kmf = km.astype(F32)
        n2 = jnp.maximum(jnp.max(jnp.sum(kf * kf, axis=-1, keepdims=True), axis=0, keepdims=True),
                         jnp.max(jnp.sum(kmf * kmf, axis=-1, keepdims=True), axis=0, keepdims=True))
        kmax_scr[...] = jnp.broadcast_to(jnp.sqrt(n2), kmax_scr.shape)

    kmax_c = kmax_scr[:, 0:1] * c
    lmin = None
    for g in range(Q_PER_KV):
        sl = slice(g * HEAD_DIM, (g + 1) * HEAD_DIM)
        q = q_ref[:, sl]
        qf = q.astype(F32)
        mc = jnp.sqrt(jnp.sum(qf * qf, axis=-1, keepdims=True)) * kmax_c
        o, l = _attn_head(q, k, v, km, vm, meta_bias, mc)
        o_ref[:, sl] = o / l
        lg = jnp.min(l)
        lmin = lg if lmin is None else jnp.minimum(lmin, lg)

    @pl.when(jnp.logical_not(lmin >= ATTN_MIN_ROW_SUM))
    def _():
        for g in range(Q_PER_KV):
            sl = slice(g * HEAD_DIM, (g + 1) * HEAD_DIM)
            o, l = _attn_head(q_ref[:, sl], k, v, km, vm, meta_bias, None)
            o_ref[:, sl] = o / l


def _attention(q, k, v, km, vm, b, s, tq):
    t, d_attn = q.shape
    wq = Q_PER_KV * HEAD_DIM
    nq = s // tq
    mp = km.shape[0]
    return pl.pallas_call(
        _attn_kernel,
        out_shape=jax.ShapeDtypeStruct((t, d_attn), F32),
        grid=(b, N_KV, nq),
        in_specs=[pl.BlockSpec((tq, wq), lambda bi, kh, qi: (bi * nq + qi, kh)),
                  pl.BlockSpec((s, HEAD_DIM), lambda bi, kh, qi: (bi, kh)),
                  pl.BlockSpec((s, HEAD_DIM), lambda bi, kh, qi: (bi, kh)),
                  pl.BlockSpec((mp, HEAD_DIM), lambda bi, kh, qi: (0, kh)),
                  pl.BlockSpec((mp, HEAD_DIM), lambda bi, kh, qi: (0, kh))],
        out_specs=pl.BlockSpec((tq, wq), lambda bi, kh, qi: (bi * nq + qi, kh)),
        scratch_shapes=[pltpu.VMEM((1, HEAD_DIM), F32)],
        compiler_params=_cparams(("parallel", "parallel", "arbitrary")),
        name="attention",
    )(q, k, v, km, vm)


def _lru_kernel(*refs, s, rc, ncast):
    xr_ref, yr_ref, xm_ref, cw_ref, cb_ref, wg_ref, bg_ref, lam_ref = refs[:8]
    cast_in = refs[8:8 + ncast]
    o_ref = refs[8 + ncast]
    cast_out = refs[9 + ncast:9 + 2 * ncast]
    xpad, af, bf, ab, bb, hfs, pfs, hbs, pbs = refs[9 + 2 * ncast:]
    l = s + N_META
    bw = LRU_BW
    for src, dst in zip(cast_in, cast_out):
        dst[...] = src[...].astype(BF16)
    xpad[0:8, :] = jnp.zeros((8, bw), F32)
    xpad[8:8 + N_META, :] = xm_ref[...]
    xpad[8 + N_META:8 + l, :] = xr_ref[...]
    xpad[8 + l:16 + l, :] = jnp.zeros((8, bw), F32)

    lam = lam_ref[...]
    neg = -lam
    sp = jnp.maximum(neg, 0.0) + jnp.log1p(jnp.exp(-jnp.abs(neg)))
    cw = cw_ref[...]
    cb = cb_ref[...]
    wg = wg_ref[...]
    bg = bg_ref[...]

    def gates(t0, n):
        xc = (cw[0:1] * xpad[pl.ds(t0 + 6, n), :] + cw[1:2] * xpad[pl.ds(t0 + 7, n), :]
              + cw[2:3] * xpad[pl.ds(t0 + 8, n), :] + cw[3:4] * xpad[pl.ds(t0 + 9, n), :]) + cb
        gt = jnp.dot(xc.astype(BF16), wg, preferred_element_type=F32) + bg
        hx = 0.5 * xc
        for d, (a_ref, b_ref) in enumerate(((af, bf), (ab, bb))):
            tr = jnp.tanh(gt[:, (2 * d) * bw:(2 * d + 1) * bw])
            ti = jnp.tanh(gt[:, (2 * d + 1) * bw:(2 * d + 2) * bw])
            hc = (-0.5 * LRU_C) * sp[d:d + 1]
            log_a = hc + hc * tr
            a = jnp.exp(log_a)
            om = 1.0 - a * a
            root = jnp.where(om > 0.0, om * lax.rsqrt(om), 0.0)
            a_ref[pl.ds(t0, n), :] = a
            b_ref[pl.ds(t0, n), :] = root * (hx + hx * ti)

    gates(0, N_META)

    def gate_body(c, carry):
        gates(pl.multiple_of(N_META + c * rc, 8), rc)
        return carry

    lax.fori_loop(0, s // rc, gate_body, 0)

    nch = 8
    cl = l // nch

    def rows(ref, i):
        return ref.at[pl.ds(i, nch, stride=cl), :]

    def pass1(i, carry):
        hf, pf, hb, pb = carry
        a = rows(af, i)[...]
        hf = a * hf + rows(bf, i)[...]
        pf = a * pf
        rows(hfs, i)[...] = hf
        rows(pfs, i)[...] = pf
        j = cl - 1 - i
        a = rows(ab, j)[...]
        hb = a * hb + rows(bb, j)[...]
        pb = a * pb
        rows(hbs, j)[...] = hb
        rows(pbs, j)[...] = pb
        return hf, pf, hb, pb

    zero = jnp.zeros((nch, bw), F32)
    one = jnp.ones((nch, bw), F32)
    hf, pf, hb, pb = lax.fori_loop(0, cl, pass1, (zero, one, zero, one), unroll=8)

    cf = [jnp.zeros((1, bw), F32)]
    for c in range(1, nch):
        cf.append(hf[c - 1:c] + pf[c - 1:c] * cf[-1])
    cin_f = jnp.concatenate(cf, axis=0)
    cbk = [jnp.zeros((1, bw), F32)]
    for c in range(nch - 2, -1, -1):
        cbk.append(hb[c + 1:c + 2] + pb[c + 1:c + 2] * cbk[-1])
    cin_b = jnp.concatenate(cbk[::-1], axis=0)

    def pass2(i, carry):
        rows(hfs, i)[...] = rows(hfs, i)[...] + rows(pfs, i)[...] * cin_f
        rows(hbs, i)[...] = rows(hbs, i)[...] + rows(pbs, i)[...] * cin_b
        return carry

    lax.fori_loop(0, cl, pass2, 0, unroll=8)

    for c in range(s // rc):
        r0 = c * rc
        hsum = hfs[N_META + r0:N_META + r0 + rc, :] + hbs[N_META + r0:N_META + r0 + rc, :]
        o_ref[r0:r0 + rc, :] = hsum * _gelu(yr_ref[r0:r0 + rc, :])


def _lru(z, zm, cw, cb, wg, bg, lam, tables, b, s, off_x, off_y, d_rnn):
    nblk = d_rnn // LRU_BW
    tspecs = []
    for tb in tables:
        assert tb.shape[0] % (b * nblk * 16) == 0
        tspecs.append(pl.BlockSpec((tb.shape[0] // (b * nblk), tb.shape[1]),
                                   lambda bi, n: (bi * nblk + n, 0)))
    rc = min(256, s)
    l = s + N_META
    bx = off_x // LRU_BW
    by = off_y // LRU_BW
    return pl.pallas_call(
        functools.partial(_lru_kernel, s=s, rc=rc, ncast=len(tables)),
        out_shape=(jax.ShapeDtypeStruct((b * s, d_rnn), F32),
                   *[jax.ShapeDtypeStruct(tb.shape, BF16) for tb in tables]),
        grid=(b, nblk),
        in_specs=[pl.BlockSpec((s, LRU_BW), lambda bi, n: (bi, bx + n)),
                  pl.BlockSpec((s, LRU_BW), lambda bi, n: (bi, by + n)),
                  pl.BlockSpec((N_META, LRU_BW), lambda bi, n: (0, bx + n)),
                  pl.BlockSpec((CONV_W, LRU_BW), lambda bi, n: (0, n)),
                  pl.BlockSpec((1, LRU_BW), lambda bi, n: (0, n)),
                  pl.BlockSpec((None, LRU_BW, 4 * LRU_BW), lambda bi, n: (n, 0, 0)),
                  pl.BlockSpec((None, 1, 4 * LRU_BW), lambda bi, n: (n, 0, 0)),
                  pl.BlockSpec((2, LRU_BW), lambda bi, n: (0, n)),
                  *tspecs],
        out_specs=(pl.BlockSpec((s, LRU_BW), lambda bi, n: (bi, n)), *tspecs),
        scratch_shapes=[pltpu.VMEM((l + 16, LRU_BW), F32)] + [pltpu.VMEM((l, LRU_BW), F32)] * 8,
        compiler_params=_cparams(("parallel", "parallel")),
        name="rg_lru",
    )(z, z, zm, cw, cb, wg, bg, lam, *tables)


def _out_proj_kernel(a_ref, r_ref, ga_ref, gr_ref, w_ref, x_ref, o_ref, m_ref):
    da = a_ref.shape[1]

    @pl.when(pl.program_id(1) == 0)
    def _():
        m_ref[:, 0:da] = _rms(a_ref[...], ga_ref[...]).astype(BF16)
        m_ref[:, da:] = _rms(r_ref[...], gr_ref[...]).astype(BF16)

    o_ref[...] = x_ref[...] + jnp.dot(m_ref[...], w_ref[...], preferred_element_type=F32)


def _out_proj(attn, lru, ga, gr, w, x2d, tm, tn):
    t, da = attn.shape
    dr = lru.shape[1]
    d = w.shape[1]
    return pl.pallas_call(
        _out_proj_kernel,
        out_shape=jax.ShapeDtypeStruct((t, d), F32),
        grid=(t // tm, d // tn),
        in_specs=[pl.BlockSpec((tm, da), lambda i, j: (i, 0)),
                  pl.BlockSpec((tm, dr), lambda i, j: (i, 0)),
                  pl.BlockSpec((1, da), lambda i, j: (0, 0)),
                  pl.BlockSpec((1, dr), lambda i, j: (0, 0)),
                  pl.BlockSpec((da + dr, tn), lambda i, j: (0, j)),
                  pl.BlockSpec((tm, tn), lambda i, j: (i, j))],
        out_specs=pl.BlockSpec((tm, tn), lambda i, j: (i, j)),
        scratch_shapes=[pltpu.VMEM((tm, da + dr), BF16)],
        compiler_params=_cparams(("parallel", "arbitrary")),
        name="out_proj",
    )(attn, lru, ga, gr, w, x2d)


def _topk_rows(sc, k):
    n, tm = sc.shape
    row = lax.broadcasted_iota(I32, (n, tm), 0)
    cur = sc
    rank = jnp.full((n, tm), k, I32)
    vals = []
    for r in range(k):
        mx = jnp.max(cur, axis=0, keepdims=True)
        idx = jnp.min(jnp.where(cur == mx, row, n), axis=0, keepdims=True)
        sel = row == idx
        rank = jnp.where(sel, r, rank)
        cur = jnp.where(sel, -jnp.inf, cur)
        vals.append(mx)
    return jnp.concatenate(vals, axis=0), rank


def _topk_rows_distinct(sc, k):
    cur = sc
    rank = jnp.full(sc.shape, k, I32)
    vals = []
    for r in range(k):
        mx = jnp.max(cur, axis=0, keepdims=True)
        sel = cur == mx
        rank = jnp.where(sel, r, rank)
        cur = jnp.where(sel, -jnp.inf, cur)
        vals.append(mx)
    return jnp.concatenate(vals, axis=0), rank


def _count_rows(x):
    return jnp.sum(x.astype(I32), axis=0, keepdims=True)


def _route_kernel(h_ref, g_ref, wq_ref, sk_ref, xn_ref, e1_ref, c1_ref, e2_ref, r2_ref,
                  q_scr, r1_s, r2_s, cnt_s, z_s):
    k = PEER_TOPK
    xn = _rms(h_ref[...], g_ref[...]).astype(BF16)
    xn_ref[...] = xn
    q = jnp.dot(xn, wq_ref[...], preferred_element_type=F32).astype(BF16)
    tm = q.shape[0]
    for c in range(2 * PEER_HEADS):
        q_scr[c] = q[:, c * N_KEYS:(c + 1) * N_KEYS]

    hk = k // 2
    pos_col = jnp.concatenate(
        [lax.broadcasted_iota(I32, (k, 1), 0)]
        + [a * k + lax.broadcasted_iota(I32, (hk, 1), 0) for a in range(1, k)], axis=0)
    nc = pos_col.shape[0]
    arow = lax.broadcasted_iota(I32, (k, tm), 0)

    def candidates(v1, v2):
        return jnp.concatenate(
            [v1[0:1] + v2] + [v1[a:a + 1] + v2[0:hk] for a in range(1, k)], axis=0)

    def head_body(h, carry):
        s1 = _dot_nt(sk_ref[2 * h], q_scr[2 * h])
        s2 = _dot_nt(sk_ref[2 * h + 1], q_scr[2 * h + 1])

        v1, r1 = _topk_rows_distinct(s1, k)
        v2, r2 = _topk_rows_distinct(s2, k)
        cand = candidates(v1, v2)
        cur = cand
        top = v1[0:1] + v2[0:1]
        zsum = jnp.zeros((1, tm), F32)
        mx = top
        for j in range(k):
            mx = jnp.max(cur, axis=0, keepdims=True)
            cur = jnp.where(cur == mx, -jnp.inf, cur)
            zsum = zsum + jnp.exp(mx - top)
        selc = cand >= mx
        cnt = jnp.concatenate(
            [_count_rows(selc[0:k])]
            + [_count_rows(selc[k + hk * (a - 1):k + hk * a]) for a in range(1, k)], axis=0)
        bad = jnp.logical_or(
            jnp.logical_or(_count_rows(r1 < k) != k, _count_rows(r2 < k) != k),
            _count_rows(cnt) != k)
        r1_s[...] = r1
        r2_s[...] = r2
        cnt_s[...] = cnt
        z_s[...] = zsum

        @pl.when(jnp.max(bad.astype(I32)) > 0)
        def _():
            v1x, r1x = _topk_rows(s1, k)
            v2x, r2x = _topk_rows(s2, k)
            pos = jnp.broadcast_to(pos_col, (nc, tm))
            curx = candidates(v1x, v2x)
            mask_a = jnp.zeros((k, tm), I32)
            zx = jnp.zeros((1, tm), F32)
            for j in range(k):
                mxx = jnp.max(curx, axis=0, keepdims=True)
                pj = jnp.min(jnp.where(curx == mxx, pos, k * k), axis=0, keepdims=True)
                curx = jnp.where(pos == pj, -jnp.inf, curx)
                zx = zx + jnp.exp(mxx - top)
                aj = jnp.right_shift(pj, 4)
                bj = jnp.bitwise_and(pj, k - 1)
                mask_a = jnp.where(arow == aj, jnp.bitwise_or(mask_a, jnp.left_shift(1, bj)), mask_a)
            r1_s[...] = r1x
            r2_s[...] = r2x
            cnt_s[...] = lax.population_count(mask_a)
            z_s[...] = zx

        r1f = r1_s[...]
        cntf = cnt_s[...]
        c1 = jnp.zeros(r1f.shape, I32)
        for a in range(k):
            c1 = jnp.where(r1f == a, cntf[a:a + 1], c1)
        e1_ref[h] = jnp.exp(s1 - v1[0:1] - jnp.log(z_s[...]))
        c1_ref[h] = c1.astype(F32)
        e2_ref[h] = jnp.exp(s2 - v2[0:1]).astype(BF16)
        r2_ref[h] = r2_s[...].astype(F32).astype(BF16)
        return carry

    lax.fori_loop(0, PEER_HEADS, head_body, 0)


def _route(h1, g, wq, sk, tm):
    t, d = h1.shape
    nq = wq.shape[1]
    rf = jax.ShapeDtypeStruct((PEER_HEADS, N_KEYS, t), F32)
    rb = jax.ShapeDtypeStruct((PEER_HEADS, N_KEYS, t), BF16)
    rspec = pl.BlockSpec((PEER_HEADS, N_KEYS, tm), lambda i: (0, 0, i))
    return pl.pallas_call(
        _route_kernel,
        out_shape=(jax.ShapeDtypeStruct((t, d), BF16), rf, rf, rb, rb),
        grid=(t // tm,),
        in_specs=[pl.BlockSpec((tm, d), lambda i: (i, 0)),
                  pl.BlockSpec((1, d), lambda i: (0, 0)),
                  pl.BlockSpec((d, nq), lambda i: (0, 0), pipeline_mode=pl.Buffered(1)),
                  pl.BlockSpec((2 * PEER_HEADS, N_KEYS, N_KEYS), lambda i: (0, 0, 0))],
        out_specs=(pl.BlockSpec((tm, d), lambda i: (i, 0)), rspec, rspec, rspec, rspec),
        scratch_shapes=[pltpu.VMEM((2 * PEER_HEADS, tm, N_KEYS), BF16),
                        pltpu.VMEM((N_KEYS, tm), I32), pltpu.VMEM((N_KEYS, tm), I32),
                        pltpu.VMEM((PEER_TOPK, tm), I32), pltpu.VMEM((1, tm), F32)],
        compiler_params=_cparams(("parallel",)),
        name="peer_route",
    )(h1, g, wq, sk)


def _gelu_rcp(x):
    k0 = -2.0 * math.sqrt(2.0 / math.pi)
    return x / (1.0 + jnp.exp(x * (k0 + (k0 * 0.044715) * (x * x))))


def _rows_bf16(x):
    t16 = jnp.broadcast_to(x, (16, x.shape[1])).astype(BF16)
    return jnp.concatenate([t16] * (N_KEYS // 16), axis=0)


def _peer_kernel(xn_ref, h_ref, u_ref, v_ref, e1_ref, c1_ref, e2_ref, r2_ref, o_ref, *, nb):
    e = pl.program_id(1)

    slab = h_ref.shape[1]
    nslab = o_ref.shape[1] // slab

    @pl.when(e == 0)
    def _():
        o_ref[:, 0:slab] = h_ref[...]
        o_ref[:, slab:] = jnp.zeros((o_ref.shape[0], o_ref.shape[1] - slab), F32)

    for cs in range(1, nslab):
        @pl.when(e == cs)
        def _(cs=cs):
            o_ref[:, cs * slab:(cs + 1) * slab] += h_ref[...]

    act = _dot_nt(u_ref[...], xn_ref[...])
    ga = _gelu_rcp(act)
    tm = ga.shape[1]
    parts = []
    for kb in range(nb):
        n1 = e * nb + kb
        w = jnp.zeros((N_KEYS, tm), BF16)
        for h in range(PEER_HEADS):
            c1 = _rows_bf16(c1_ref[h, pl.ds(n1, 1), :])
            e1 = _rows_bf16(e1_ref[h, pl.ds(n1, 1), :])
            w = w + jnp.where(r2_ref[h] < c1, e1 * e2_ref[h], jnp.zeros_like(e1))
        parts.append((w.astype(F32) * ga[kb * N_KEYS:(kb + 1) * N_KEYS, :]).T.astype(BF16))
    wa = jnp.concatenate(parts, axis=1)
    o_ref[...] += jnp.dot(wa, v_ref[...], preferred_element_type=F32)


def _peer(xnt, h1, u, v, e1, c1, e2, r2, tm, ec):
    t, d = xnt.shape
    nchunk = u.shape[0] // ec
    nb = ec // N_KEYS
    nslab = 4
    assert nchunk >= nslab and d % (nslab * 128) == 0
    rspec = pl.BlockSpec((PEER_HEADS, N_KEYS, tm), lambda i, e: (0, 0, i))
    return pl.pallas_call(
        functools.partial(_peer_kernel, nb=nb),
        out_shape=jax.ShapeDtypeStruct((t, d), F32),
        grid=(t // tm, nchunk),
        in_specs=[pl.BlockSpec((tm, d), lambda i, e: (i, 0)),
                  pl.BlockSpec((tm, d // nslab), lambda i, e: (i, jnp.minimum(e, nslab - 1))),
                  pl.BlockSpec((ec, d), lambda i, e: (e, 0)),
                  pl.BlockSpec((ec, d), lambda i, e: (e, 0)),
                  rspec, rspec, rspec, rspec],
        out_specs=pl.BlockSpec((tm, d), lambda i, e: (i, 0)),
        compiler_params=_cparams(("parallel", "arbitrary")),
        name="peer_experts",
    )(xnt, h1, u, v, e1, c1, e2, r2)


def _rope_tables(s):
    rows = s // GRID_W
    row = jnp.repeat(jnp.arange(rows, dtype=F32), GRID_W)
    col = (jnp.arange(rows * GRID_W) % GRID_W).astype(F32)
    inv = ROPE_THETA ** (-jnp.arange(ROPE_PAIRS, dtype=F32) / ROPE_PAIRS)
    ar = row[:, None] * inv[None, :]
    ac = col[:, None] * inv[None, :]
    ang = jnp.concatenate([ar, ar, ac, ac], axis=-1)
    lane = jnp.arange(HEAD_DIM)
    sign = jnp.where((lane % (2 * ROPE_PAIRS)) < ROPE_PAIRS, -1.0, 1.0).astype(F32)
    return jnp.cos(ang), jnp.sin(ang) * sign[None, :]


def _tile(n, pref):
    t = min(pref, n)
    while n % t:
        t //= 2
    return t


def kernel(x, meta_tokens, norm1_g, w_in, q_norm_g, k_norm_g, conv_w, conv_b, w_rg, b_rg, w_ig,
           b_ig, lru_lambda, attn_out_g, lru_out_g, w_out, norm2_g, peer_wq, peer_subkeys,
           peer_u, peer_v):
    b, s, d = x.shape
    t = b * s
    depth = w_in.shape[0]
    assert depth == 1, "meta rows of the stream are only materialised as attention / recurrence context"
    d_attn = attn_out_g.shape[-1]
    d_rnn = lru_out_g.shape[-1]
    d_kv = N_KV * HEAD_DIM
    off_x = d_attn + 2 * d_kv
    off_y = off_x + d_rnn
    nblk = d_rnn // LRU_BW

    x2d = x.reshape(t, d)
    meta = meta_tokens.astype(x.dtype)
    g1 = norm1_g[0].reshape(1, d)
    w_in_b = w_in[0].astype(BF16)

    tm = _tile(t, 512)
    z = _in_proj(x2d, g1, w_in_b, tm, 1024)
    zm = _in_proj(meta, g1, w_in_b, N_META, 1024)

    cos, sin_s = _rope_tables(s)
    qg = q_norm_g[0].reshape(1, HEAD_DIM)
    kg = k_norm_g[0].reshape(1, HEAD_DIM)
    q, k, v = _qk_prep(z, cos, sin_s, qg, kg, _tile(s, 256), d_attn)
    _, km, vm = _qk_prep(zm, jnp.ones((N_META, HEAD_DIM), F32), jnp.zeros((N_META, HEAD_DIM), F32),
                         qg, kg, N_META, d_attn)
    pad = ((0, N_KEYS - N_META), (0, 0))
    attn = _attention(q, k, v, jnp.pad(km, pad), jnp.pad(vm, pad), b, s, _tile(s, 1024))

    wg = jnp.concatenate([w_rg[0, 0], w_ig[0, 0], w_rg[0, 1], w_ig[0, 1]], axis=-1)
    wg = (0.5 * wg).astype(BF16)
    bg = 0.5 * jnp.stack([b_rg[0, 0], b_ig[0, 0], b_rg[0, 1], b_ig[0, 1]], axis=0)
    bg = bg.reshape(4, nblk, LRU_BW).transpose(1, 0, 2).reshape(nblk, 1, 4 * LRU_BW)
    lru, u_b, v_b, w_out_b, wq_b = _lru(
        z, zm, conv_w[0], conv_b[0].reshape(1, d_rnn), wg, bg, lru_lambda[0],
        (peer_u[0], peer_v[0], w_out[0], peer_wq[0]), b, s, off_x, off_y, d_rnn)

    h1 = _out_proj(attn, lru, attn_out_g[0].reshape(1, d_attn), lru_out_g[0].reshape(1, d_rnn),
                   w_out_b, x2d, tm, 1024)

    sk = peer_subkeys[0].reshape(2 * PEER_HEADS, N_KEYS, -1).astype(BF16)
    xn2, e1, c1, e2, r2 = _route(h1, norm2_g[0].reshape(1, d), wq_b, sk,
                                 _tile(t, 256))
    out = _peer(xn2, h1, u_b, v_b, e1, c1, e2, r2,
                _tile(t, 512), 512)
    return out.reshape(b, s, d)
```
